```python
import math
import jax, jax.numpy as jnp
from jax import lax
import numpy as np

D_MODEL = 1024
BATCH = 8
SEQ = 4096
DEPTH = 1

M_HEADS = 4
M_QK = 128
M_V = 256
M_CHUNK = 128
A_HEADS = 8
A_DH = 64
A_DV = 2 * A_DH
Q_BLOCK = 128
ROPE_THETA = 10000.0
D_FF = 2816
CONV_W = 3
EPS = 1e-6

M_QK_W = M_HEADS * M_QK
M_V_W = M_HEADS * M_V
N_GATE = 4 * M_HEADS
A_QK_W = A_HEADS * 2 * A_DH
A_V_W = A_HEADS * A_DV
SPLITS = (M_QK_W, M_QK_W, M_V_W, M_V_W, N_GATE, A_QK_W, A_QK_W, A_V_W, D_MODEL, D_MODEL)
D_IN = sum(SPLITS)

kernel_name = "hybrid_mlstm_diffattn_convffn_encoder"


def rmsnorm(x, g):
    x32 = x.astype(jnp.float32)
    r = x32 * lax.rsqrt(jnp.mean(x32 * x32, axis=-1, keepdims=True) + EPS)
    return (r * g).astype(x.dtype)


def rope_tables(seq, dim):
    pos = jnp.arange(seq, dtype=jnp.float32)
    inv = ROPE_THETA ** (-jnp.arange(0, dim, 2, dtype=jnp.float32) / dim)
    ang = pos[:, None] * inv[None, :]
    return jnp.cos(ang), jnp.sin(ang)


def apply_rope(t, cos, sin):
    t1, t2 = jnp.split(t, 2, axis=-1)
    out = jnp.concatenate([t1 * cos - t2 * sin, t1 * sin + t2 * cos], axis=-1)
    return out.astype(t.dtype)


def mlstm_chunkwise(q, k, v, i_pre, f_pre):
    B, H, S, dk = q.shape
    dv = v.shape[-1]
    L = M_CHUNK
    nc = S // L
    q = q.reshape(B, H, nc, L, dk)
    k = k.reshape(B, H, nc, L, dk)
    v = v.reshape(B, H, nc, L, dv)
    i_pre = i_pre.reshape(B, H, nc, L)
    logf = jax.nn.log_sigmoid(f_pre.reshape(B, H, nc, L))
    b = jnp.cumsum(logf, axis=-1)
    g = b[..., -1]
    a = g[..., None] - b + i_pre
    m_loc = jnp.max(a, axis=-1)
    w = jnp.exp(a - m_loc[..., None])
    kw = k * w[..., None]
    kv = jnp.einsum('bhcld,bhcle->bhcde', kw, v)
    ksum = jnp.sum(kw, axis=3)

    def step(carry, xs):
        C, n, m = carry
        kv_c, ks_c, g_c, ml_c = xs
        m_new = jnp.maximum(g_c + m, ml_c)
        a1 = jnp.exp(g_c + m - m_new)
        a2 = jnp.exp(ml_c - m_new)
        C_new = a1[..., None, None] * C + a2[..., None, None] * kv_c
        n_new = a1[..., None] * n + a2[..., None] * ks_c
        return (C_new, n_new, m_new), (C, n, m)

    init = (jnp.zeros((B, H, dk, dv), jnp.float32), jnp.zeros((B, H, dk), jnp.float32),
            jnp.zeros((B, H), jnp.float32))
    xs = (jnp.moveaxis(kv, 2, 0), jnp.moveaxis(ksum, 2, 0), jnp.moveaxis(g, 2, 0), jnp.moveaxis(m_loc, 2, 0))
    _, (C_prev, n_prev, m_prev) = lax.scan(step, init, xs)
    C_prev = jnp.moveaxis(C_prev, 0, 2)
    n_prev = jnp.moveaxis(n_prev, 0, 2)
    m_prev = jnp.moveaxis(m_prev, 0, 2)

    D = b[..., :, None] - b[..., None, :] + i_pre[..., None, :]
    mask = jnp.tril(jnp.ones((L, L), dtype=bool))
    D = jnp.where(mask, D, -jnp.inf)
    inter_log = b + m_prev[..., None]
    m_t = jnp.maximum(jnp.max(D, axis=-1), inter_log)
    s = jnp.einsum('bhcld,bhcsd->bhcls', q, k) * jnp.exp(D - m_t[..., None])
    inter_w = jnp.exp(inter_log - m_t)
    num = jnp.einsum('bhcls,bhcse->bhcle', s, v) + inter_w[..., None] * jnp.einsum('bhcld,bhcde->bhcle', q, C_prev)
    den = jnp.sum(s, axis=-1) + inter_w * jnp.einsum('bhcld,bhcd->bhcl', q, n_prev)
    h = num / jnp.maximum(jnp.abs(den), jnp.exp(-m_t))[..., None]
    return h.reshape(B, H, S, dv)


def token_mixer(xn, w_in, gate_bias, m_out_norm, q_norm, k_norm, lq1, lk1, lq2, lk2,
                a_out_norm, p_a, p_b, w_o, cos, sin, lam_init):
    B, S, _ = xn.shape
    f32 = jnp.float32
    proj = xn @ w_in
    mq, mk, mv, mo, mg, aq, ak, av, ga, gb = jnp.split(proj, np.cumsum(SPLITS)[:-1].tolist(), axis=-1)

    def heads(t, h, d):
        return t.reshape(B, S, h, d).transpose(0, 2, 1, 3)

    q = heads(mq, M_HEADS, M_QK).astype(f32)
    k = heads(mk, M_HEADS, M_QK).astype(f32) * (M_QK ** -0.5)
    v = heads(mv, M_HEADS, M_V).astype(f32)
    gates = (mg.astype(f32) + gate_bias.astype(f32)).reshape(B, S, 4, M_HEADS).transpose(2, 0, 3, 1)
    h_fwd = mlstm_chunkwise(q, k, v, gates[0], gates[1])
    flip = lambda t: jnp.flip(t, axis=2)
    h_bwd = flip(mlstm_chunkwise(flip(q), flip(k), flip(v), flip(gates[2]), flip(gates[3])))
    hm = (h_fwd + h_bwd).transpose(0, 2, 1, 3)
    hm = rmsnorm(hm, m_out_norm.reshape(M_HEADS, M_V)).reshape(B, S, M_V_W)
    h_a = (hm * jax.nn.sigmoid(mo.astype(f32))).astype(xn.dtype)

    qa = aq.reshape(B, S, A_HEADS, 2, A_DH).transpose(0, 2, 3, 1, 4)
    ka = ak.reshape(B, S, A_HEADS, 2, A_DH).transpose(0, 2, 3, 1, 4)
    qa = apply_rope(rmsnorm(qa, q_norm), cos, sin) * (A_DH ** -0.5)
    ka = apply_rope(rmsnorm(ka, k_norm), cos, sin)
    va = heads(av, A_HEADS, A_DV)
    lam = (jnp.exp(jnp.sum(lq1.astype(f32) * lk1.astype(f32))) -
           jnp.exp(jnp.sum(lq2.astype(f32) * lk2.astype(f32))) + lam_init)
    nb = S // Q_BLOCK
    qb = jnp.moveaxis(qa.reshape(B, A_HEADS, 2, nb, Q_BLOCK, A_DH), 3, 0)

    def block(qi):
        sc = jnp.einsum('bhcqd,bhckd->bhcqk', qi, ka).astype(f32)
        p = jax.nn.softmax(sc, axis=-1)
        wgt = (p[:, :, 0] - lam * p[:, :, 1]).astype(va.dtype)
        return jnp.einsum('bhqk,bhkv->bhqv', wgt, va)

    o = lax.map(block, qb)
    o = o.transpose(1, 0, 3, 2, 4).reshape(B, S, A_HEADS, A_DV)
    o = rmsnorm(o, a_out_norm) * (1.0 - lam_init)
    h_b = o.reshape(B, S, A_V_W).astype(xn.dtype)

    y = jax.nn.sigmoid(ga) * (h_a @ p_a) + jax.nn.sigmoid(gb) * (h_b @ p_b)
    return y @ w_o


def conv_ffn(xn, w_up, conv_w, conv_b, w_down):
    u = xn @ w_up
    u = lax.conv_general_dilated(u, conv_w, window_strides=(1,), padding='SAME',
                                 dimension_numbers=('NWC', 'WIO', 'NWC'),
                                 feature_group_count=u.shape[-1]) + conv_b
    a, g = jnp.split(u, 2, axis=-1)
    return (jax.nn.gelu(g) * a) @ w_down


def setup_inputs(seed: int = 0) -> dict:
    key = jax.random.key(seed)
    ks = jax.random.split(key, 24)
    f32 = jnp.float32

    def nrm(k, shape, scale):
        return jax.random.normal(k, shape, f32) * scale

    fbias = jnp.linspace(3.0, 6.0, M_HEADS, dtype=f32)
    gb_noise = nrm(ks[3], (DEPTH, 4, M_HEADS), 0.1)
    gate_bias = (gb_noise + jnp.stack([jnp.zeros_like(fbias), fbias, jnp.zeros_like(fbias), fbias])[None]).reshape(DEPTH, N_GATE)
    return {
        "x": nrm(ks[0], (BATCH, SEQ, D_MODEL), 1.0),
        "norm1": 1.0 + nrm(ks[1], (DEPTH, D_MODEL), 0.02),
        "w_in": nrm(ks[2], (DEPTH, D_MODEL, D_IN), D_MODEL ** -0.5),
        "gate_bias": gate_bias,
        "m_out_norm": 1.0 + nrm(ks[4], (DEPTH, M_V_W), 0.02),
        "q_norm": 1.0 + nrm(ks[5], (DEPTH, A_DH), 0.02),
        "k_norm": 1.0 + nrm(ks[6], (DEPTH, A_DH), 0.02),
        "lam_q1": nrm(ks[7], (DEPTH, A_DH), 0.1),
        "lam_k1": nrm(ks[8], (DEPTH, A_DH), 0.1),
        "lam_q2": nrm(ks[9], (DEPTH, A_DH), 0.1),
        "lam_k2": nrm(ks[10], (DEPTH, A_DH), 0.1),
        "a_out_norm": 1.0 + nrm(ks[11], (DEPTH, A_DV), 0.02),
        "p_a": nrm(ks[12], (DEPTH, M_V_W, D_MODEL), M_V_W ** -0.5),
        "p_b": nrm(ks[13], (DEPTH, A_V_W, D_MODEL), A_V_W ** -0.5),
        "w_o": nrm(ks[14], (DEPTH, D_MODEL, D_MODEL), D_MODEL ** -0.5),
        "norm2": 1.0 + nrm(ks[15], (DEPTH, D_MODEL), 0.02),
        "w_up": nrm(ks[16], (DEPTH, D_MODEL, 2 * D_FF), D_MODEL ** -0.5),
        "conv_w": nrm(ks[17], (DEPTH, CONV_W, 1, 2 * D_FF), CONV_W ** -0.5),
        "conv_b": nrm(ks[18], (DEPTH, 2 * D_FF), 0.02),
        "w_down": nrm(ks[19], (DEPTH, D_FF, D_MODEL), D_FF ** -0.5),
    }


def reference(x, norm1, w_in, gate_bias, m_out_norm, q_norm, k_norm, lam_q1, lam_k1, lam_q2, lam_k2,
              a_out_norm, p_a, p_b, w_o, norm2, w_up, conv_w, conv_b, w_down):
    S = x.shape[1]
    cos, sin = rope_tables(S, A_DH)
    for l in range(DEPTH):
        lam_init = 0.8 - 0.6 * math.exp(-0.3 * l)
        xn = rmsnorm(x, norm1[l])
        x = x + token_mixer(xn, w_in[l], gate_bias[l], m_out_norm[l], q_norm[l], k_norm[l],
                            lam_q1[l], lam_k1[l], lam_q2[l], lam_k2[l], a_out_norm[l],
                            p_a[l], p_b[l], w_o[l], cos, sin, lam_init)
        xn = rmsnorm(x, norm2[l])
        x = x + conv_ffn(xn, w_up[l], conv_w[l], conv_b[l], w_down[l])
    return x
```

```python
import functools
import math

import numpy as np
import jax
import jax.numpy as jnp
from jax import lax
from jax.experimental import pallas as pl
from jax.experimental.pallas import tpu as pltpu

F32 = jnp.float32
BF16 = jnp.bfloat16

D_MODEL = 1024
M_HEADS = 4
M_QK = 128
M_V = 256
M_CHUNK = 128
A_HEADS = 8
A_DH = 64
A_DV = 128
ROPE_THETA = 10000.0
D_FF = 2816
CONV_W = 3
EPS = 1e-6
N_GATE = 4 * M_HEADS

LANES = 128
SUBLANES = 8
VMEM_LIMIT = 60 * 1024 * 1024

_SRC_MQ, _SRC_MG, _SRC_AQ, _SRC_AK, _SRC_AV = 0, 3072, 3088, 4112, 5136
D_IN = 8208
OFF_MQ, OFF_MK, OFF_MV, OFF_MO = 0, 512, 1024, 2048
OFF_AQ, OFF_AK, OFF_AV, OFF_GA, OFF_GB = 3072, 4096, 5120, 6144, 7168
D_WIDE = 8192


def _head_lane_source():
    lane = np.arange(LANES)
    half, rem = np.divmod(lane, 64)
    comp, freq = np.divmod(rem, 32)
    return comp * 64 + half * 32 + freq, half, comp, freq


def _wide_columns():
    src, _, _, _ = _head_lane_source()
    qk = (np.arange(A_HEADS)[:, None] * LANES + src[None, :]).reshape(-1)
    return np.concatenate([np.arange(0, _SRC_MG), _SRC_AQ + qk, _SRC_AK + qk, np.arange(_SRC_AV, D_IN)])


def _inproj_body(x_ref, g_ref, w_ref, wg_ref, bg_ref, o_ref, og_ref, xn_ref):
    @pl.when(pl.program_id(1) == 0)
    def _():
        x = x_ref[...]
        ms = jnp.mean(x * x, axis=-1, keepdims=True)
        xn = (x * lax.rsqrt(ms + EPS) * g_ref[...]).astype(BF16)
        xn_ref[...] = xn
        og_ref[...] = jnp.dot(xn, wg_ref[...], preferred_element_type=F32) + bg_ref[...]

    o_ref[...] = jnp.dot(xn_ref[...], w_ref[...], preferred_element_type=F32).astype(BF16)


def _inproj(x2, g, w, wg, bg, tm, tn):
    t = x2.shape[0]
    return pl.pallas_call(
        _inproj_body,
        grid=(t // tm, D_WIDE // tn),
        in_specs=[
            pl.BlockSpec((tm, D_MODEL), lambda i, j: (i, 0)),
            pl.BlockSpec((1, D_MODEL), lambda i, j: (0, 0)),
            pl.BlockSpec((D_MODEL, tn), lambda i, j: (0, j)),
            pl.BlockSpec((D_MODEL, LANES), lambda i, j: (0, 0)),
            pl.BlockSpec((1, LANES), lambda i, j: (0, 0)),
        ],
        out_specs=[
            pl.BlockSpec((tm, tn), lambda i, j: (i, j)),
            pl.BlockSpec((tm, LANES), lambda i, j: (i, 0)),
        ],
        out_shape=[
            jax.ShapeDtypeStruct((t, D_WIDE), BF16),
            jax.ShapeDtypeStruct((t, LANES), F32),
        ],
        scratch_shapes=[pltpu.VMEM((tm, D_MODEL), BF16)],
        compiler_params=pltpu.CompilerParams(
            dimension_semantics=("parallel", "arbitrary"), vmem_limit_bytes=VMEM_LIMIT),
        name="inproj",
    )(x2, g, w, wg, bg)


def _lane_scan(x, op, ident, reverse):
    n = x.shape[1]
    lane = lax.broadcasted_iota(jnp.int32, x.shape, 1)
    sh = 1
    while sh < n:
        if reverse:
            y = jnp.where(lane < n - sh, pltpu.roll(x, n - sh, 1), ident)
        else:
            y = jnp.where(lane >= sh, pltpu.roll(x, sh, 1), ident)
        x = op(x, y)
        sh *= 2
    return x


_ROW_R, _ROW_G, _ROW_ML = 0, 1, 2
_COL_CM, _COL_B, _COL_W = 0, 32, 64


def _mlstm_body(q_ref, k_ref, v_ref, mo_ref, gt_ref, nw_ref, o_ref,
                hf_ref, hb_ref, c_ref, n_ref, rows_ref, cols_ref, *, seq):
    L = M_CHUNK
    nc = seq // L
    scale = M_QK ** -0.5
    gt = gt_ref[0, 0]

    for d in range(2):
        reverse = d == 1
        i_pre = gt[2 * d]
        logf = jax.nn.log_sigmoid(gt[2 * d + 1])
        b = _lane_scan(logf, jnp.add, 0.0, reverse)
        g = jnp.sum(logf, axis=1, keepdims=True)
        a = g - b + i_pre
        m_loc = jnp.max(a, axis=1, keepdims=True)
        w = jnp.exp(a - m_loc) * scale
        r = i_pre - b
        cm = _lane_scan(r, jnp.maximum, -jnp.inf, reverse)
        rows_ref[d, _ROW_R * nc:(_ROW_R + 1) * nc, :] = r
        rows_ref[d, _ROW_G * nc:(_ROW_G + 1) * nc, :] = jnp.broadcast_to(g, (nc, L))
        rows_ref[d, _ROW_ML * nc:(_ROW_ML + 1) * nc, :] = jnp.broadcast_to(m_loc, (nc, L))
        pad = [jnp.zeros((32 - nc, L), F32)] if nc < 32 else []
        stack = jnp.concatenate([cm] + pad + [b] + pad + [w] + pad + [jnp.zeros((32, L), F32)], axis=0)
        cols_ref[d] = stack.T

    c_ref[...] = jnp.zeros(c_ref.shape, F32)
    n_ref[...] = jnp.zeros(n_ref.shape, F32)

    row_i = lax.broadcasted_iota(jnp.int32, (L, L), 0)
    col_i = lax.broadcasted_iota(jnp.int32, (L, L), 1)
    masks = (col_i <= row_i, col_i >= row_i)
    ones = jnp.ones((L, L), BF16)
    contract_rows = (((0,), (0,)), ((), ()))
    contract_last = (((1,), (1,)), ((), ()))

    def chunk(d, c, m):
        rs = pl.multiple_of(c * L, L)
        q = q_ref[pl.ds(rs, L), :]
        k = k_ref[pl.ds(rs, L), :]
        v = v_ref[pl.ds(rs, L), :]
        t = pltpu.roll(cols_ref[d], jnp.bitwise_and(LANES - c, LANES - 1), 1)
        cm_col = t[:, _COL_CM:_COL_CM + 1]
        b_col = t[:, _COL_B:_COL_B + 1]
        w_col = t[:, _COL_W:_COL_W + 1]
        r_row = rows_ref[d, pl.ds(_ROW_R * nc + c, 1), :]
        g_row = rows_ref[d, pl.ds(_ROW_G * nc + c, 1), :]
        ml_row = rows_ref[d, pl.ds(_ROW_ML * nc + c, 1), :]

        u = jnp.maximum(cm_col, m)
        wgt = jnp.exp(jnp.where(masks[d], r_row - u, -jnp.inf))
        qk = lax.dot_general(q, k, contract_last, preferred_element_type=F32)
        s = ((qk * scale) * wgt).astype(BF16)
        inter_w = jnp.exp(m - u)
        c_prev = c_ref[d]
        n_prev = n_ref[d]
        num = (jnp.dot(s, v, preferred_element_type=F32)
               + jnp.concatenate([inter_w, inter_w], axis=1)
               * jnp.dot(q, c_prev.astype(BF16), preferred_element_type=F32))
        den = (jnp.dot(s, ones, preferred_element_type=F32)
               + inter_w * jnp.dot(q, n_prev.astype(BF16), preferred_element_type=F32))
        rinv = 1.0 / jnp.maximum(jnp.abs(den), jnp.exp(-(b_col + u)))
        h = num * jnp.concatenate([rinv, rinv], axis=1)

        kw = (k.astype(F32) * w_col).astype(BF16)
        kv = lax.dot_general(kw, v, contract_rows, preferred_element_type=F32)
        kn = lax.dot_general(kw, ones, contract_rows, preferred_element_type=F32)
        m_new = jnp.maximum(g_row + m, ml_row)
        a1 = jnp.exp(g_row + m - m_new)[:, 0:1]
        a2 = jnp.exp(ml_row - m_new)[:, 0:1]
        c_ref[d] = a1 * c_prev + a2 * kv
        n_ref[d] = a1 * n_prev + a2 * kn
        return h, m_new

    def step(c, carry):
        m_f, m_b = carry
        cb = nc - 1 - c
        h_f, m_f = chunk(0, c, m_f)
        h_b, m_b = chunk(1, cb, m_b)
        hf_ref[pl.ds(pl.multiple_of(c * L, L), L), :] = h_f
        hb_ref[pl.ds(pl.multiple_of(cb * L, L), L), :] = h_b
        return m_f, m_b

    m0 = jnp.zeros((1, L), F32)
    lax.fori_loop(0, nc, step, (m0, m0))

    def finish(c, carry):
        rows = pl.ds(pl.multiple_of(c * L, L), L)
        hm = hf_ref[rows, :] + hb_ref[rows, :]
        ms = jnp.mean(hm * hm, axis=-1, keepdims=True)
        y = hm * lax.rsqrt(ms + EPS) * nw_ref[0]
        o_ref[rows, :] = (y * jax.nn.sigmoid(mo_ref[rows, :].astype(F32))).astype(BF16)
        return carry

    lax.fori_loop(0, nc, finish, 0)


def _mlstm(wide, gates_t, m_out_norm, batch, seq):
    nc = seq // M_CHUNK
    t = batch * seq
    body = functools.partial(_mlstm_body, seq=seq)
    return pl.pallas_call(
        body,
        grid=(batch, M_HEADS),
        in_specs=[
            pl.BlockSpec((seq, M_QK), lambda b, h: (b, OFF_MQ // M_QK + h)),
            pl.BlockSpec((seq, M_QK), lambda b, h: (b, OFF_MK // M_QK + h)),
            pl.BlockSpec((seq, M_V), lambda b, h: (b, OFF_MV // M_V + h)),
            pl.BlockSpec((seq, M_V), lambda b, h: (b, OFF_MO // M_V + h)),
            pl.BlockSpec((1, 1, 4, nc, M_CHUNK), lambda b, h: (b, h, 0, 0, 0)),
            pl.BlockSpec((1, 1, M_V), lambda b, h: (h, 0, 0)),
        ],
        out_specs=pl.BlockSpec((seq, M_V), lambda b, h: (b, h)),
        out_shape=jax.ShapeDtypeStruct((t, M_HEADS * M_V), BF16),
        scratch_shapes=[
            pltpu.VMEM((seq, M_V), F32),
            pltpu.VMEM((seq, M_V), F32),
            pltpu.VMEM((2, M_QK, M_V), F32),
            pltpu.VMEM((2, M_QK, LANES), F32),
            pltpu.VMEM((2, 3 * nc, M_CHUNK), F32),
            pltpu.VMEM((2, M_CHUNK, LANES), F32),
        ],
        compiler_params=pltpu.CompilerParams(
            dimension_semantics=("parallel", "parallel"), vmem_limit_bytes=VMEM_LIMIT),
        name="mlstm",
    )(wide, wide, wide, wide, gates_t, m_out_norm)


def _qk_prep(x, nw, cosf, sinf, gmat):
    ms = jnp.dot((x * x).astype(BF16), gmat, preferred_element_type=F32)
    xn = x * lax.rsqrt(ms + EPS) * nw
    return xn * cosf + pltpu.roll(xn, LANES // 2, 1) * sinf


def _attn_body(aq_ref, ak_ref, av_ref, cosq_ref, sinq_ref, cosk_ref, sink_ref, qn_ref, kn_ref,
               gm_ref, cmask_ref, lam_ref, on_ref, o_ref, kt_ref, *, seq, lam_init, kc):
    @pl.when(pl.program_id(2) == 0)
    def _():
        for i in range(seq // kc):
            rows = slice(i * kc, (i + 1) * kc)
            kk = _qk_prep(ak_ref[rows, :].astype(F32), kn_ref[...], cosk_ref[rows, :], sink_ref[rows, :],
                          gm_ref[...])
            kt_ref[:, rows] = kk.T.astype(BF16)

    lv = lam_ref[...]
    lam = (jnp.exp(jnp.sum(lv[0:1] * lv[1:2], axis=1, keepdims=True))
           - jnp.exp(jnp.sum(lv[2:3] * lv[3:4], axis=1, keepdims=True)) + lam_init)

    q = _qk_prep(aq_ref[...].astype(F32), qn_ref[...], cosq_ref[...], sinq_ref[...], gm_ref[...])
    q = q * (A_DH ** -0.5)
    kt = kt_ref[...]
    probs = []
    for c in range(2):
        qc = (q * cmask_ref[c:c + 1, :]).astype(BF16)
        s = jnp.dot(qc, kt, preferred_element_type=F32)
        p = jnp.exp(s - jnp.max(s, axis=-1, keepdims=True))
        probs.append((p, jnp.sum(p, axis=-1, keepdims=True)))
    (p1, l1), (p2, l2) = probs
    wgt = (p1 * (1.0 / l1) - p2 * (lam / l2)).astype(BF16)
    o = jnp.dot(wgt, av_ref[...], preferred_element_type=F32)
    ms = jnp.mean(o * o, axis=-1, keepdims=True)
    o_ref[...] = (o * lax.rsqrt(ms + EPS) * on_ref[...] * (1.0 - lam_init)).astype(BF16)


def _attention(wide, tabs, qn, kn, gmat, cmask, lamv, on, batch, seq, tq, lam_init):
    nq = seq // tq
    t = batch * seq
    cosf, sinf = tabs
    body = functools.partial(_attn_body, seq=seq, lam_init=lam_init, kc=min(512, seq))
    const = lambda b, h, i: (0, 0)
    return pl.pallas_call(
        body,
        grid=(batch, A_HEADS, nq),
        in_specs=[
            pl.BlockSpec((tq, LANES), lambda b, h, i: (b * nq + i, OFF_AQ // LANES + h)),
            pl.BlockSpec((seq, LANES), lambda b, h, i: (b, OFF_AK // LANES + h)),
            pl.BlockSpec((seq, LANES), lambda b, h, i: (b, OFF_AV // LANES + h)),
            pl.BlockSpec((tq, LANES), lambda b, h, i: (i, 0)),
            pl.BlockSpec((tq, LANES), lambda b, h, i: (i, 0)),
            pl.BlockSpec((seq, LANES), const),
            pl.BlockSpec((seq, LANES), const),
            pl.BlockSpec((1, LANES), const),
            pl.BlockSpec((1, LANES), const),
            pl.BlockSpec((LANES, LANES), const),
            pl.BlockSpec((2, LANES), const),
            pl.BlockSpec((4, A_DH), const),
            pl.BlockSpec((1, A_DV), const),
        ],
        out_specs=pl.BlockSpec((tq, A_DV), lambda b, h, i: (b * nq + i, h)),
        out_shape=jax.ShapeDtypeStruct((t, A_HEADS * A_DV), BF16),
        scratch_shapes=[pltpu.VMEM((LANES, seq), BF16)],
        compiler_params=pltpu.CompilerParams(
            dimension_semantics=("parallel", "parallel", "arbitrary"), vmem_limit_bytes=VMEM_LIMIT),
        name="diffattn",
    )(wide, wide, wide, cosf, sinf, cosf, sinf, qn, kn, gmat, cmask, lamv, on)


def _merge_body(x_ref, ha_ref, hb_ref, ga_ref, gb_ref, pa_ref, pb_ref, wo_ref, o_ref):
    ya = jnp.dot(ha_ref[...], pa_ref[...], preferred_element_type=F32)
    yb = jnp.dot(hb_ref[...], pb_ref[...], preferred_element_type=F32)
    y = (jax.nn.sigmoid(ga_ref[...].astype(F32)) * ya + jax.nn.sigmoid(gb_ref[...].astype(F32)) * yb)
    o_ref[...] = x_ref[...] + jnp.dot(y.astype(BF16), wo_ref[...], preferred_element_type=F32)


def _merge(x2, ha, hb, wide, pa, pb, wo, tm):
    t = x2.shape[0]
    row = lambda i: (i, 0)
    const = lambda i: (0, 0)
    wspec = pl.BlockSpec((D_MODEL, D_MODEL), const)
    return pl.pallas_call(
        _merge_body,
        grid=(t // tm,),
        in_specs=[
            pl.BlockSpec((tm, D_MODEL), row),
            pl.BlockSpec((tm, D_MODEL), row),
            pl.BlockSpec((tm, D_MODEL), row),
            pl.BlockSpec((tm, D_MODEL), lambda i: (i, OFF_GA // D_MODEL)),
            pl.BlockSpec((tm, D_MODEL), lambda i: (i, OFF_GB // D_MODEL)),
            wspec, wspec, wspec,
        ],
        out_specs=pl.BlockSpec((tm, D_MODEL), row),
        out_shape=jax.ShapeDtypeStruct((t, D_MODEL), F32),
        compiler_params=pltpu.CompilerParams(
            dimension_semantics=("parallel",), vmem_limit_bytes=VMEM_LIMIT),
        name="merge",
    )(x2, ha, hb, wide, wide, pa, pb, wo)


def _ffn_body(xc_ref, xp_ref, xn_ref, g_ref, wup_ref, cw_ref, cb_ref, wdn_ref, o_ref, *, ts, fc):
    i = pl.program_id(1)
    last = pl.num_programs(1) - 1

    def nrm(x):
        ms = jnp.mean(x * x, axis=-1, keepdims=True)
        return x * lax.rsqrt(ms + EPS) * g_ref[...]

    xprev = nrm(xp_ref[...]) * (i > 0).astype(F32)
    xnext = nrm(xn_ref[...]) * (i < last).astype(F32)
    xe = jnp.concatenate([xprev, nrm(xc_ref[...]), xnext], axis=0).astype(BF16)
    h0 = SUBLANES

    def conv(u, col):
        cw = cw_ref[:, col]
        return (u[h0 - 1:h0 - 1 + ts] * cw[0:1] + u[h0:h0 + ts] * cw[1:2]
                + u[h0 + 1:h0 + 1 + ts] * cw[2:3] + cb_ref[:, col])

    acc = xc_ref[...]
    for c in range(D_FF // fc):
        col_a = slice(c * fc, (c + 1) * fc)
        col_g = slice(D_FF + c * fc, D_FF + (c + 1) * fc)
        ua = jnp.dot(xe, wup_ref[:, col_a], preferred_element_type=F32)
        ug = jnp.dot(xe, wup_ref[:, col_g], preferred_element_type=F32)
        hcol = (jax.nn.gelu(conv(ug, col_g)) * conv(ua, col_a)).astype(BF16)
        acc = acc + jnp.dot(hcol, wdn_ref[col_a, :], preferred_element_type=F32)
    o_ref[...] = acc


def _ffn(x1, g, wup, cw, cb, wdn, batch, seq, ts):
    t = batch * seq
    ns = seq // ts
    hb = ts // SUBLANES
    nblk8 = t // SUBLANES
    body = functools.partial(_ffn_body, ts=ts, fc=256)
    const = lambda b, i: (0, 0)
    return pl.pallas_call(
        body,
        grid=(batch, ns),
        in_specs=[
            pl.BlockSpec((ts, D_MODEL), lambda b, i: (b * ns + i, 0)),
            pl.BlockSpec((SUBLANES, D_MODEL), lambda b, i: (jnp.maximum((b * ns + i) * hb - 1, 0), 0)),
            pl.BlockSpec((SUBLANES, D_MODEL), lambda b, i: (jnp.minimum((b * ns + i + 1) * hb, nblk8 - 1), 0)),
            pl.BlockSpec((1, D_MODEL), const),
            pl.BlockSpec((D_MODEL, 2 * D_FF), const),
            pl.BlockSpec((CONV_W, 2 * D_FF), const),
            pl.BlockSpec((1, 2 * D_FF), const),
            pl.BlockSpec((D_FF, D_MODEL), const),
        ],
        out_specs=pl.BlockSpec((ts, D_MODEL), lambda b, i: (b * ns + i, 0)),
        out_shape=jax.ShapeDtypeStruct((t, D_MODEL), F32),
        compiler_params=pltpu.CompilerParams(
            dimension_semantics=("parallel", "arbitrary"), vmem_limit_bytes=VMEM_LIMIT),
        name="convffn",
    )(x1, x1, x1, g, wup, cw, cb, wdn)


def _rope_lane_tables(seq):
    _, half, _, freq = _head_lane_source()
    pos = jnp.arange(seq, dtype=F32)
    inv = ROPE_THETA ** (-jnp.arange(0, A_DH, 2, dtype=F32) / A_DH)
    ang = pos[:, None] * inv[None, :]
    cos, sin = jnp.cos(ang), jnp.sin(ang)
    sign = jnp.asarray(np.where(half == 0, -1.0, 1.0), F32)
    return cos[:, freq], sin[:, freq] * sign[None, :]


def kernel(x, norm1, w_in, gate_bias, m_out_norm, q_norm, k_norm, lam_q1, lam_k1, lam_q2, lam_k2,
           a_out_norm, p_a, p_b, w_o, norm2, w_up, conv_w, conv_b, w_down):
    batch, seq, _ = x.shape
    t = batch * seq
    depth = norm1.shape[0]
    nc = seq // M_CHUNK
    tm = min(1024, t)
    tq = min(256, seq)
    ts = min(512, seq)

    _, half, comp, freq = _head_lane_source()
    lane_norm = half * 32 + freq
    gmat = jnp.asarray((comp[:, None] == comp[None, :]).astype(np.float32) / A_DH, BF16)
    cmask = jnp.asarray(np.stack([comp == 0, comp == 1]).astype(np.float32))
    tabs = _rope_lane_tables(seq)
    cols = _wide_columns()

    x2 = x.reshape(t, D_MODEL)
    for l in range(depth):
        lam_init = 0.8 - 0.6 * math.exp(-0.3 * l)
        w_wide = jnp.take(w_in[l], cols, axis=1).astype(BF16)
        w_gate = jnp.pad(w_in[l][:, _SRC_MG:_SRC_MG + N_GATE], ((0, 0), (0, LANES - N_GATE))).astype(BF16)
        b_gate = jnp.pad(gate_bias[l].astype(F32), (0, LANES - N_GATE)).reshape(1, LANES)

        wide, gates = _inproj(x2, norm1[l].reshape(1, D_MODEL), w_wide, w_gate, b_gate, tm, min(1024, D_WIDE))
        gates_t = (gates[:, :N_GATE].reshape(batch, seq, 4, M_HEADS).transpose(0, 3, 2, 1)
                   .reshape(batch, M_HEADS, 4, nc, M_CHUNK))

        h_a = _mlstm(wide, gates_t, m_out_norm[l].reshape(M_HEADS, 1, M_V), batch, seq)

        lamv = jnp.stack([lam_q1[l], lam_k1[l], lam_q2[l], lam_k2[l]]).astype(F32)
        h_b = _attention(wide, tabs, q_norm[l][lane_norm].reshape(1, LANES), k_norm[l][lane_norm].reshape(1, LANES),
                         gmat, cmask, lamv, a_out_norm[l].reshape(1, A_DV), batch, seq, tq, lam_init)

        x2 = _merge(x2, h_a, h_b, wide, p_a[l].astype(BF16), p_b[l].astype(BF16), w_o[l].astype(BF16), tm)

        x2 = _ffn(x2, norm2[l].reshape(1, D_MODEL), w_up[l].astype(BF16),
                  conv_w[l].reshape(CONV_W, 2 * D_FF), conv_b[l].reshape(1, 2 * D_FF),
                  w_down[l].astype(BF16), batch, seq, ts)
    return x2.reshape(batch, seq, D_MODEL)
```

```python
import functools
import math

import numpy as np
import jax
import jax.numpy as jnp
from jax import lax
from jax.experimental import pallas as pl
from jax.experimental.pallas import tpu as pltpu

F32 = jnp.float32
BF16 = jnp.bfloat16

D_MODEL = 1024
M_HEADS = 4
M_QK = 128
M_V = 256
M_CHUNK = 128
A_HEADS = 8
A_DH = 64
A_DV = 128
ROPE_THETA = 10000.0
D_FF = 2816
CONV_W = 3
EPS = 1e-6
N_GATE = 4 * M_HEADS

LANES = 128
SUBLANES = 8
VMEM_LIMIT = 60 * 1024 * 1024

_SRC_MQ, _SRC_MG, _SRC_AQ, _SRC_AK, _SRC_AV = 0, 3072, 3088, 4112, 5136
D_IN = 8208
OFF_MQ, OFF_MK, OFF_MV, OFF_MO = 0, 512, 1024, 2048
OFF_AQ, OFF_AK, OFF_AV, OFF_GA, OFF_GB = 3072, 4096, 5120, 6144, 7168
D_WIDE = 8192


def _head_lane_source():
    lane = np.arange(LANES)
    half, rem = np.divmod(lane, 64)
    comp, freq = np.divmod(rem, 32)
    return comp * 64 + half * 32 + freq, half, comp, freq


def _wide_columns():
    src, _, _, _ = _head_lane_source()
    qk = (np.arange(A_HEADS)[:, None] * LANES + src[None, :]).reshape(-1)
    return np.concatenate([np.arange(0, _SRC_MG), _SRC_AQ + qk, _SRC_AK + qk, np.arange(_SRC_AV, D_IN)])


def _inproj_body(x_ref, g_ref, w_ref, wg_ref, bg_ref, o_ref, og_ref, xn_ref):
    @pl.when(pl.program_id(1) == 0)
    def _():
        x = x_ref[...]
        ms = jnp.mean(x * x, axis=-1, keepdims=True)
        xn = (x * lax.rsqrt(ms + EPS) * g_ref[...]).astype(BF16)
        xn_ref[...] = xn
        og_ref[...] = jnp.dot(xn, wg_ref[...], preferred_element_type=F32) + bg_ref[...]

    o_ref[...] = jnp.dot(xn_ref[...], w_ref[...], preferred_element_type=F32).astype(BF16)


def _inproj(x2, g, w, wg, bg, tm, tn):
    t = x2.shape[0]
    return pl.pallas_call(
        _inproj_body,
        grid=(t // tm, D_WIDE // tn),
        in_specs=[
            pl.BlockSpec((tm, D_MODEL), lambda i, j: (i, 0)),
            pl.BlockSpec((1, D_MODEL), lambda i, j: (0, 0)),
            pl.BlockSpec((D_MODEL, tn), lambda i, j: (0, j)),
            pl.BlockSpec((D_MODEL, LANES), lambda i, j: (0, 0)),
            pl.BlockSpec((1, LANES), lambda i, j: (0, 0)),
        ],
        out_specs=[
            pl.BlockSpec((tm, tn), lambda i, j: (i, j)),
            pl.BlockSpec((tm, LANES), lambda i, j: (i, 0)),
        ],
        out_shape=[
            jax.ShapeDtypeStruct((t, D_WIDE), BF16),
            jax.ShapeDtypeStruct((t, LANES), F32),
        ],
        scratch_shapes=[pltpu.VMEM((tm, D_MODEL), BF16)],
        compiler_params=pltpu.CompilerParams(
            dimension_semantics=("parallel", "arbitrary"), vmem_limit_bytes=VMEM_LIMIT),
        name="inproj",
    )(x2, g, w, wg, bg)


def _lane_scan(x, op, ident, reverse):
    n = x.shape[1]
    lane = lax.broadcasted_iota(jnp.int32, x.shape, 1)
    sh = 1
    while sh < n:
        if reverse:
            y = jnp.where(lane < n - sh, pltpu.roll(x, n - sh, 1), ident)
        else:
            y = jnp.where(lane >= sh, pltpu.roll(x, sh, 1), ident)
        x = op(x, y)
        sh *= 2
    return x


_ROW_R, _ROW_G, _ROW_ML = 0, 1, 2
_COL_CM, _COL_B, _COL_W = 0, 32, 64


def _mlstm_body(q_ref, k_ref, v_ref, mo_ref, gt_ref, nw_ref, o_ref,
                hf_ref, hb_ref, c_ref, n_ref, rows_ref, cols_ref, *, seq):
    L = M_CHUNK
    nc = seq // L
    scale = M_QK ** -0.5
    gt = gt_ref[0, 0]

    for d in range(2):
        reverse = d == 1
        i_pre = gt[2 * d]
        logf = jax.nn.log_sigmoid(gt[2 * d + 1])
        b = _lane_scan(logf, jnp.add, 0.0, reverse)
        g = jnp.sum(logf, axis=1, keepdims=True)
        a = g - b + i_pre
        m_loc = jnp.max(a, axis=1, keepdims=True)
        w = jnp.exp(a - m_loc) * scale
        r = i_pre - b
        cm = _lane_scan(r, jnp.maximum, -jnp.inf, reverse)
        rows_ref[d, _ROW_R * nc:(_ROW_R + 1) * nc, :] = r
        rows_ref[d, _ROW_G * nc:(_ROW_G + 1) * nc, :] = jnp.broadcast_to(g, (nc, L))
        rows_ref[d, _ROW_ML * nc:(_ROW_ML + 1) * nc, :] = jnp.broadcast_to(m_loc, (nc, L))
        pad = [jnp.zeros((32 - nc, L), F32)] if nc < 32 else []
        stack = jnp.concatenate([cm] + pad + [b] + pad + [w] + pad + [jnp.zeros((32, L), F32)], axis=0)
        cols_ref[d] = stack.T

    c_ref[...] = jnp.zeros(c_ref.shape, F32)
    n_ref[...] = jnp.zeros(n_ref.shape, F32)

    row_i = lax.broadcasted_iota(jnp.int32, (L, L), 0)
    col_i = lax.broadcasted_iota(jnp.int32, (L, L), 1)
    masks = (col_i <= row_i, col_i >= row_i)
    ones = jnp.ones((L, L), BF16)
    contract_rows = (((0,), (0,)), ((), ()))
    contract_last = (((1,), (1,)), ((), ()))

    def chunk(d, c, m):
        rs = pl.multiple_of(c * L, L)
        q = q_ref[pl.ds(rs, L), :]
        k = k_ref[pl.ds(rs, L), :]
        v = v_ref[pl.ds(rs, L), :]
        t = pltpu.roll(cols_ref[d], jnp.bitwise_and(LANES - c, LANES - 1), 1)
        cm_col = t[:, _COL_CM:_COL_CM + 1]
        b_col = t[:, _COL_B:_COL_B + 1]
        w_col = t[:, _COL_W:_COL_W + 1]
        r_row = rows_ref[d, pl.ds(_ROW_R * nc + c, 1), :]
        g_row = rows_ref[d, pl.ds(_ROW_G * nc + c, 1), :]
        ml_row = rows_ref[d, pl.ds(_ROW_ML * nc + c, 1), :]

        u = jnp.maximum(cm_col, m)
        wgt = jnp.exp(jnp.where(masks[d], r_row - u, -jnp.inf))
        qk = lax.dot_general(q, k, contract_last, preferred_element_type=F32)
        s = ((qk * scale) * wgt).astype(BF16)
        inter_w = jnp.exp(m - u)
        c_prev = c_ref[d]
        n_prev = n_ref[d]
        num = (jnp.dot(s, v, preferred_element_type=F32)
               + jnp.concatenate([inter_w, inter_w], axis=1)
               * jnp.dot(q, c_prev.astype(BF16), preferred_element_type=F32))
        den = (jnp.dot(s, ones, preferred_element_type=F32)
               + inter_w * jnp.dot(q, n_prev.astype(BF16), preferred_element_type=F32))
        rinv = 1.0 / jnp.maximum(jnp.abs(den), jnp.exp(-(b_col + u)))
        h = num * jnp.concatenate([rinv, rinv], axis=1)

        kw = (k.astype(F32) * w_col).astype(BF16)
        kv = lax.dot_general(kw, v, contract_rows, preferred_element_type=F32)
        kn = lax.dot_general(kw, ones, contract_rows, preferred_element_type=F32)
        m_new = jnp.maximum(g_row + m, ml_row)
        a1 = jnp.exp(g_row + m - m_new)[:, 0:1]
        a2 = jnp.exp(ml_row - m_new)[:, 0:1]
        c_ref[d] = a1 * c_prev + a2 * kv
        n_ref[d] = a1 * n_prev + a2 * kn
        return h, m_new

    def step(c, carry):
        m_f, m_b = carry
        cb = nc - 1 - c
        h_f, m_f = chunk(0, c, m_f)
        h_b, m_b = chunk(1, cb, m_b)
        hf_ref[pl.ds(pl.multiple_of(c * L, L), L), :] = h_f
        hb_ref[pl.ds(pl.multiple_of(cb * L, L), L), :] = h_b
        return m_f, m_b

    m0 = jnp.zeros((1, L), F32)
    lax.fori_loop(0, nc, step, (m0, m0))

    def finish(c, carry):
        rows = pl.ds(pl.multiple_of(c * L, L), L)
        hm = hf_ref[rows, :] + hb_ref[rows, :]
        ms = jnp.mean(hm * hm, axis=-1, keepdims=True)
        y = hm * lax.rsqrt(ms + EPS) * nw_ref[0]
        o_ref[rows, :] = (y * jax.nn.sigmoid(mo_ref[rows, :].astype(F32))).astype(BF16)
        return carry

    lax.fori_loop(0, nc, finish, 0)


def _mlstm(wide, gates_t, m_out_norm, batch, seq):
    nc = seq // M_CHUNK
    t = batch * seq
    body = functools.partial(_mlstm_body, seq=seq)
    return pl.pallas_call(
        body,
        grid=(batch, M_HEADS),
        in_specs=[
            pl.BlockSpec((seq, M_QK), lambda b, h: (b, OFF_MQ // M_QK + h)),
            pl.BlockSpec((seq, M_QK), lambda b, h: (b, OFF_MK // M_QK + h)),
            pl.BlockSpec((seq, M_V), lambda b, h: (b, OFF_MV // M_V + h)),
            pl.BlockSpec((seq, M_V), lambda b, h: (b, OFF_MO // M_V + h)),
            pl.BlockSpec((1, 1, 4, nc, M_CHUNK), lambda b, h: (b, h, 0, 0, 0)),
            pl.BlockSpec((1, 1, M_V), lambda b, h: (h, 0, 0)),
        ],
        out_specs=pl.BlockSpec((seq, M_V), lambda b, h: (b, h)),
        out_shape=jax.ShapeDtypeStruct((t, M_HEADS * M_V), BF16),
        scratch_shapes=[
            pltpu.VMEM((seq, M_V), F32),
            pltpu.VMEM((seq, M_V), F32),
            pltpu.VMEM((2, M_QK, M_V), F32),
            pltpu.VMEM((2, M_QK, LANES), F32),
            pltpu.VMEM((2, 3 * nc, M_CHUNK), F32),
            pltpu.VMEM((2, M_CHUNK, LANES), F32),
        ],
        compiler_params=pltpu.CompilerParams(
            dimension_semantics=("parallel", "parallel"), vmem_limit_bytes=VMEM_LIMIT),
        name="mlstm",
    )(wide, wide, wide, wide, gates_t, m_out_norm)


def _qk_prep(xb, ctab, stab, gmat, pmat):
    x = xb.astype(F32)
    ms = jnp.dot((x * x).astype(BF16), gmat, preferred_element_type=F32)
    xp = jnp.dot(xb, pmat, preferred_element_type=F32)
    return (x * ctab + xp * stab) * lax.rsqrt(ms + EPS)


V_ROWS = A_DV + 16
LOG2E = math.log2(math.e)
SAFE_BOUND_LOG2 = 48.0
BOUND_SLACK = 1.02


def _attn_body(aq_ref, ak_ref, av_ref, cq_ref, sq_ref, ck_ref, sk_ref, nw_ref, gm_ref, pm_ref, cmaskc_ref,
               lam_ref, on_ref, o_ref, kp_ref, vt_ref, qt_ref, m_ref, ot_ref, *, seq, tq, kc, lam_init):
    gm = gm_ref[...]
    pm = pm_ref[...]
    prep = min(512, seq)
    tiles_per_prep = prep // tq
    nq = seq // tq

    for i in range(seq // prep):
        rows = slice(i * prep, (i + 1) * prep)
        kp_ref[rows, :] = _qk_prep(ak_ref[rows, :], ck_ref[rows, :], sk_ref[rows, :], gm, pm).astype(BF16)
        vt_ref[0:A_DV, rows] = av_ref[rows, :].T
    vt_ref[A_DV:V_ROWS, :] = jnp.ones((V_ROWS - A_DV, seq), BF16)

    for i in range(seq // prep):
        rows = slice(i * prep, (i + 1) * prep)
        qt = _qk_prep(aq_ref[rows, :], cq_ref[rows, :], sq_ref[rows, :], gm, pm).astype(BF16).T
        for c in range(2):
            qct = qt * cmaskc_ref[:, c:c + 1]
            for k in range(tiles_per_prep):
                qt_ref[i * tiles_per_prep + k, c] = qct[:, k * tq:(k + 1) * tq]

    nw = jnp.abs(nw_ref[...])
    bound = (jnp.max(nw[0:1], axis=1, keepdims=True) * jnp.max(nw[1:2], axis=1, keepdims=True)
             * (A_DH * A_DH ** -0.5 * LOG2E * BOUND_SLACK))
    m_ref[...] = jnp.broadcast_to(bound.reshape(1, 1, 1), m_ref.shape)

    @pl.when(bound[0, 0] > SAFE_BOUND_LOG2)
    def _():
        def exact_max(qi, carry):
            for c in range(2):
                mx = jnp.full((1, tq), -jnp.inf, F32)
                for j in range(seq // kc):
                    s = jnp.dot(kp_ref[j * kc:(j + 1) * kc, :], qt_ref[qi, c], preferred_element_type=F32)
                    mx = jnp.maximum(mx, jnp.max(s, axis=0, keepdims=True))
                m_ref[qi, c:c + 1, :] = mx
            return carry

        lax.fori_loop(0, nq, exact_max, 0)

    lv = lam_ref[...]
    lam = (jnp.exp(jnp.sum(lv[0:1] * lv[1:2], axis=1, keepdims=True))
           - jnp.exp(jnp.sum(lv[2:3] * lv[3:4], axis=1, keepdims=True)) + lam_init)

    def q_tile(qi, carry):
        acc = [jnp.zeros((V_ROWS, tq), F32) for _ in range(2)]
        for j in range(seq // kc):
            kblk = kp_ref[j * kc:(j + 1) * kc, :]
            vblk = vt_ref[:, j * kc:(j + 1) * kc]
            for c in range(2):
                s = jnp.dot(kblk, qt_ref[qi, c], preferred_element_type=F32)
                p = jnp.exp2(s - m_ref[qi, c:c + 1, :]).astype(BF16)
                acc[c] = acc[c] + jnp.dot(vblk, p, preferred_element_type=F32)
        ot_ref[qi] = (acc[0][0:A_DV] * (1.0 / acc[0][A_DV:A_DV + 1])
                      - acc[1][0:A_DV] * (lam / acc[1][A_DV:A_DV + 1]))
        return carry

    lax.fori_loop(0, seq // tq, q_tile, 0, unroll=2)

    for i in range(seq // tq):
        ot = ot_ref[i]
        ms = jnp.mean(ot * ot, axis=0, keepdims=True)
        y = ot * lax.rsqrt(ms + EPS) * (on_ref[...] * (1.0 - lam_init))
        o_ref[i * tq:(i + 1) * tq, :] = y.astype(BF16).T


def _attention(wide, qtabs, ktabs, nw, gmat, pmat, cmaskc, lamv, on, batch, seq, tq, lam_init):
    t = batch * seq
    body = functools.partial(_attn_body, seq=seq, tq=tq, kc=min(1024, seq), lam_init=lam_init)
    const = lambda b, h: (0, 0)
    table = pl.BlockSpec((seq, LANES), const)
    return pl.pallas_call(
        body,
        grid=(batch, A_HEADS),
        in_specs=[
            pl.BlockSpec((seq, LANES), lambda b, h: (b, OFF_AQ // LANES + h)),
            pl.BlockSpec((seq, LANES), lambda b, h: (b, OFF_AK // LANES + h)),
            pl.BlockSpec((seq, LANES), lambda b, h: (b, OFF_AV // LANES + h)),
            table, table, table, table,
            pl.BlockSpec((2, LANES), const),
            pl.BlockSpec((LANES, LANES), const),
            pl.BlockSpec((LANES, LANES), const),
            pl.BlockSpec((LANES, 2), const),
            pl.BlockSpec((4, A_DH), const),
            pl.BlockSpec((A_DV, 1), const),
        ],
        out_specs=pl.BlockSpec((seq, A_DV), lambda b, h: (b, h)),
        out_shape=jax.ShapeDtypeStruct((t, A_HEADS * A_DV), BF16),
        scratch_shapes=[
            pltpu.VMEM((seq, LANES), BF16),
            pltpu.VMEM((V_ROWS, seq), BF16),
            pltpu.VMEM((seq // tq, 2, LANES, tq), BF16),
            pltpu.VMEM((seq // tq, SUBLANES, tq), F32),
            pltpu.VMEM((seq // tq, A_DV, tq), F32),
        ],
        compiler_params=pltpu.CompilerParams(
            dimension_semantics=("parallel", "parallel"), vmem_limit_bytes=VMEM_LIMIT),
        name="diffattn",
    )(wide, wide, wide, qtabs[0], qtabs[1], ktabs[0], ktabs[1], nw, gmat, pmat, cmaskc, lamv, on)


def _merge_body(x_ref, ha_ref, hb_ref, ga_ref, gb_ref, pa_ref, pb_ref, wo_ref, o_ref):
    ya = jnp.dot(ha_ref[...], pa_ref[...], preferred_element_type=F32)
    yb = jnp.dot(hb_ref[...], pb_ref[...], preferred_element_type=F32)
    y = (jax.nn.sigmoid(ga_ref[...].astype(F32)) * ya + jax.nn.sigmoid(gb_ref[...].astype(F32)) * yb)
    o_ref[...] = x_ref[...] + jnp.dot(y.astype(BF16), wo_ref[...], preferred_element_type=F32)


def _merge(x2, ha, hb, wide, pa, pb, wo, tm):
    t = x2.shape[0]
    row = lambda i: (i, 0)
    const = lambda i: (0, 0)
    wspec = pl.BlockSpec((D_MODEL, D_MODEL), const)
    return pl.pallas_call(
        _merge_body,
        grid=(t // tm,),
        in_specs=[
            pl.BlockSpec((tm, D_MODEL), row),
            pl.BlockSpec((tm, D_MODEL), row),
            pl.BlockSpec((tm, D_MODEL), row),
            pl.BlockSpec((tm, D_MODEL), lambda i: (i, OFF_GA // D_MODEL)),
            pl.BlockSpec((tm, D_MODEL), lambda i: (i, OFF_GB // D_MODEL)),
            wspec, wspec, wspec,
        ],
        out_specs=pl.BlockSpec((tm, D_MODEL), row),
        out_shape=jax.ShapeDtypeStruct((t, D_MODEL), F32),
        compiler_params=pltpu.CompilerParams(
            dimension_semantics=("parallel",), vmem_limit_bytes=VMEM_LIMIT),
        name="merge",
    )(x2, ha, hb, wide, wide, pa, pb, wo)


def _ffn_body(xc_ref, xp_ref, xn_ref, g_ref, wup_ref, cw_ref, cb_ref, wdn_ref, o_ref, *, ts, fc):
    i = pl.program_id(1)
    last = pl.num_programs(1) - 1

    def nrm(x):
        ms = jnp.mean(x * x, axis=-1, keepdims=True)
        return x * lax.rsqrt(ms + EPS) * g_ref[...]

    xprev = nrm(xp_ref[...]) * (i > 0).astype(F32)
    xnext = nrm(xn_ref[...]) * (i < last).astype(F32)
    xe = jnp.concatenate([xprev, nrm(xc_ref[...]), xnext], axis=0).astype(BF16)
    h0 = SUBLANES

    def conv(u, col):
        cw = cw_ref[:, col]
        return (u[h0 - 1:h0 - 1 + ts] * cw[0:1] + u[h0:h0 + ts] * cw[1:2]
                + u[h0 + 1:h0 + 1 + ts] * cw[2:3] + cb_ref[:, col])

    acc = xc_ref[...]
    for c in range(D_FF // fc):
        col_a = slice(c * fc, (c + 1) * fc)
        col_g = slice(D_FF + c * fc, D_FF + (c + 1) * fc)
        ua = jnp.dot(xe, wup_ref[:, col_a], preferred_element_type=F32)
        ug = jnp.dot(xe, wup_ref[:, col_g], preferred_element_type=F32)
        hcol = (jax.nn.gelu(conv(ug, col_g)) * conv(ua, col_a)).astype(BF16)
        acc = acc + jnp.dot(hcol, wdn_ref[col_a, :], preferred_element_type=F32)
    o_ref[...] = acc


def _ffn(x1, g, wup, cw, cb, wdn, batch, seq, ts):
    t = batch * seq
    ns = seq // ts
    hb = ts // SUBLANES
    nblk8 = t // SUBLANES
    body = functools.partial(_ffn_body, ts=ts, fc=256)
    const = lambda b, i: (0, 0)
    return pl.pallas_call(
        body,
        grid=(batch, ns),
        in_specs=[
            pl.BlockSpec((ts, D_MODEL), lambda b, i: (b * ns + i, 0)),
            pl.BlockSpec((SUBLANES, D_MODEL), lambda b, i: (jnp.maximum((b * ns + i) * hb - 1, 0), 0)),
            pl.BlockSpec((SUBLANES, D_MODEL), lambda b, i: (jnp.minimum((b * ns + i + 1) * hb, nblk8 - 1), 0)),
            pl.BlockSpec((1, D_MODEL), const),
            pl.BlockSpec((D_MODEL, 2 * D_FF), const),
            pl.BlockSpec((CONV_W, 2 * D_FF), const),
            pl.BlockSpec((1, 2 * D_FF), const),
            pl.BlockSpec((D_FF, D_MODEL), const),
        ],
        out_specs=pl.BlockSpec((ts, D_MODEL), lambda b, i: (b * ns + i, 0)),
        out_shape=jax.ShapeDtypeStruct((t, D_MODEL), F32),
        compiler_params=pltpu.CompilerParams(
            dimension_semantics=("parallel", "arbitrary"), vmem_limit_bytes=VMEM_LIMIT),
        name="convffn",
    )(x1, x1, x1, g, wup, cw, cb, wdn)


def _rope_lane_tables(seq):
    _, half, _, freq = _head_lane_source()
    pos = jnp.arange(seq, dtype=F32)
    inv = ROPE_THETA ** (-jnp.arange(0, A_DH, 2, dtype=F32) / A_DH)
    ang = pos[:, None] * inv[None, :]
    cos, sin = jnp.cos(ang), jnp.sin(ang)
    sign = jnp.asarray(np.where(half == 0, -1.0, 1.0), F32)
    return cos[:, freq], sin[:, freq] * sign[None, :]


def kernel(x, norm1, w_in, gate_bias, m_out_norm, q_norm, k_norm, lam_q1, lam_k1, lam_q2, lam_k2,
           a_out_norm, p_a, p_b, w_o, norm2, w_up, conv_w, conv_b, w_down):
    batch, seq, _ = x.shape
    t = batch * seq
    depth = norm1.shape[0]
    nc = seq // M_CHUNK
    tm = min(1024, t)
    tq = min(256, seq)
    ts = min(512, seq)

    _, half, comp, freq = _head_lane_source()
    lane_norm = np.stack([half * 32 + freq, (1 - half) * 32 + freq])
    gmat = jnp.asarray((comp[:, None] == comp[None, :]).astype(np.float32) / A_DH, BF16)
    lane = np.arange(LANES)
    pmat = jnp.asarray((lane[:, None] == (lane[None, :] + LANES // 2) % LANES).astype(np.float32), BF16)
    cmaskc = jnp.asarray(np.stack([comp == 0, comp == 1], axis=1).astype(np.float32), BF16)
    tabs = _rope_lane_tables(seq)
    cols = _wide_columns()

    x2 = x.reshape(t, D_MODEL)
    for l in range(depth):
        lam_init = 0.8 - 0.6 * math.exp(-0.3 * l)
        w_wide = jnp.take(w_in[l], cols, axis=1).astype(BF16)
        w_gate = jnp.pad(w_in[l][:, _SRC_MG:_SRC_MG + N_GATE], ((0, 0), (0, LANES - N_GATE))).astype(BF16)
        b_gate = jnp.pad(gate_bias[l].astype(F32), (0, LANES - N_GATE)).reshape(1, LANES)

        wide, gates = _inproj(x2, norm1[l].reshape(1, D_MODEL), w_wide, w_gate, b_gate, tm, min(1024, D_WIDE))
        gates_t = (gates[:, :N_GATE].reshape(batch, seq, 4, M_HEADS).transpose(0, 3, 2, 1)
                   .reshape(batch, M_HEADS, 4, nc, M_CHUNK))

        h_a = _mlstm(wide, gates_t, m_out_norm[l].reshape(M_HEADS, 1, M_V), batch, seq)

        lamv = jnp.stack([lam_q1[l], lam_k1[l], lam_q2[l], lam_k2[l]]).astype(F32)
        qw, kw = q_norm[l][lane_norm].astype(F32), k_norm[l][lane_norm].astype(F32)
        qscale = A_DH ** -0.5 * LOG2E
        qtabs = (tabs[0] * (qw[0] * qscale), tabs[1] * (qw[1] * qscale))
        ktabs = (tabs[0] * kw[0], tabs[1] * kw[1])
        h_b = _attention(wide, qtabs, ktabs, jnp.stack([qw[0], kw[0]]), gmat, pmat, cmaskc, lamv,
                         a_out_norm[l].reshape(A_DV, 1), batch, seq, tq, lam_init)

        x2 = _merge(x2, h_a, h_b, wide, p_a[l].astype(BF16), p_b[l].astype(BF16), w_o[l].astype(BF16), tm)

        x2 = _ffn(x2, norm2[l].reshape(1, D_MODEL), w_up[l].astype(BF16),
                  conv_w[l].reshape(CONV_W, 2 * D_FF), conv_b[l].reshape(1, 2 * D_FF),
                  w_down[l].astype(BF16), batch, seq, ts)
    return x2.reshape(batch, seq, D_MODEL)
```

```python
import functools
import math

import numpy as np
import jax
import jax.numpy as jnp
from jax import lax
from jax.experimental import pallas as pl
from jax.experimental.pallas import tpu as pltpu

F32 = jnp.float32
BF16 = jnp.bfloat16

D_MODEL = 1024
M_HEADS = 4
M_QK = 128
M_V = 256
M_CHUNK = 128
A_HEADS = 8
A_DH = 64
A_DV = 128
ROPE_THETA = 10000.0
D_FF = 2816
CONV_W = 3
EPS = 1e-6
N_GATE = 4 * M_HEADS

LANES = 128
SUBLANES = 8
VMEM_LIMIT = 60 * 1024 * 1024

_SRC_MG = 3072
OFF_MQ, OFF_MK, OFF_MV, OFF_MO = 0, 512, 1024, 2048
OFF_AQ, OFF_AK, OFF_AV, OFF_GA, OFF_GB = 3072, 4096, 5120, 6144, 7168
D_WIDE = 8192


def _head_lanes():
    lane = np.arange(LANES)
    comp, rem = np.divmod(lane, A_DH)
    half, freq = np.divmod(rem, A_DH // 2)
    partner = comp * A_DH + (1 - half) * (A_DH // 2) + freq
    return half, comp, freq, partner


def _inproj_body(x_ref, g_ref, w_ref, wg_ref, bg_ref, o_ref, ogt_ref, xn_ref, *, tn):
    x = x_ref[...]
    ms = jnp.mean(x * x, axis=-1, keepdims=True)
    xn = (x * lax.rsqrt(ms + EPS) * g_ref[...]).astype(BF16)
    xn_ref[...] = xn
    og = jnp.dot(xn, wg_ref[...], preferred_element_type=F32) + bg_ref[...]
    ogt_ref[...] = og.T
    for n in range(D_WIDE // tn):
        cols = slice(n * tn, (n + 1) * tn)
        o_ref[:, cols] = jnp.dot(xn_ref[...], w_ref[:, cols], preferred_element_type=F32).astype(BF16)


def _inproj(x2, g, w, wg, bg, tm, tn):
    t = x2.shape[0]
    const = lambda i: (0, 0)
    return pl.pallas_call(
        functools.partial(_inproj_body, tn=tn),
        grid=(t // tm,),
        in_specs=[
            pl.BlockSpec((tm, D_MODEL), lambda i: (i, 0)),
            pl.BlockSpec((1, D_MODEL), const),
            pl.BlockSpec((D_MODEL, D_WIDE), const, pipeline_mode=pl.Buffered(1)),
            pl.BlockSpec((D_MODEL, LANES), const),
            pl.BlockSpec((1, LANES), const),
        ],
        out_specs=[
            pl.BlockSpec((tm, D_WIDE), lambda i: (i, 0)),
            pl.BlockSpec((LANES, tm), lambda i: (0, i)),
        ],
        out_shape=[
            jax.ShapeDtypeStruct((t, D_WIDE), BF16),
            jax.ShapeDtypeStruct((LANES, t), F32),
        ],
        scratch_shapes=[pltpu.VMEM((tm, D_MODEL), BF16)],
        compiler_params=pltpu.CompilerParams(
            dimension_semantics=("parallel",), vmem_limit_bytes=VMEM_LIMIT),
        name="inproj",
    )(x2, g, w, wg, bg)


def _lane_scan(x, op, ident, reverse):
    n = x.shape[1]
    lane = lax.broadcasted_iota(jnp.int32, x.shape, 1)
    sh = 1
    while sh < n:
        if reverse:
            y = jnp.where(lane < n - sh, pltpu.roll(x, n - sh, 1), ident)
        else:
            y = jnp.where(lane >= sh, pltpu.roll(x, sh, 1), ident)
        x = op(x, y)
        sh *= 2
    return x


_ROW_R, _ROW_G, _ROW_ML = 0, 1, 2
_COL_CM, _COL_B, _COL_W = 0, 32, 64


def _mlstm_body(q_ref, k_ref, v_ref, mo_ref, gt_ref, nw_ref, o_ref,
                hf_ref, hb_ref, rows_ref, cols_ref, *, seq):
    L = M_CHUNK
    nc = seq // L
    scale = M_QK ** -0.5
    gt = gt_ref[0, 0]

    for d in range(2):
        reverse = d == 1
        i_pre = gt[2 * d]
        logf = jax.nn.log_sigmoid(gt[2 * d + 1])
        b = _lane_scan(logf, jnp.add, 0.0, reverse)
        g = jnp.sum(logf, axis=1, keepdims=True)
        a = g - b + i_pre
        m_loc = jnp.max(a, axis=1, keepdims=True)
        w = jnp.exp(a - m_loc) * scale
        r = i_pre - b
        cm = _lane_scan(r, jnp.maximum, -jnp.inf, reverse)
        rows_ref[d, _ROW_R * nc:(_ROW_R + 1) * nc, :] = r
        rows_ref[d, _ROW_G * nc:(_ROW_G + 1) * nc, :] = jnp.broadcast_to(g, (nc, L))
        rows_ref[d, _ROW_ML * nc:(_ROW_ML + 1) * nc, :] = jnp.broadcast_to(m_loc, (nc, L))
        pad = [jnp.zeros((32 - nc, L), F32)] if nc < 32 else []
        stack = jnp.concatenate([cm] + pad + [b] + pad + [w] + pad + [jnp.zeros((32, L), F32)], axis=0)
        cols_ref[d] = stack.T

    row_i = lax.broadcasted_iota(jnp.int32, (L, L), 0)
    col_i = lax.broadcasted_iota(jnp.int32, (L, L), 1)
    masks = (col_i <= row_i, col_i >= row_i)
    ones = jnp.ones((L, L), BF16)
    contract_rows = (((0,), (0,)), ((), ()))
    contract_last = (((1,), (1,)), ((), ()))

    def chunk(d, c, state):
        m, c_prev, n_prev = state
        rs = pl.multiple_of(c * L, L)
        q = q_ref[pl.ds(rs, L), :]
        k = k_ref[pl.ds(rs, L), :]
        v = v_ref[pl.ds(rs, L), :]
        t = pltpu.roll(cols_ref[d], jnp.bitwise_and(LANES - c, LANES - 1), 1)
        cm_col = t[:, _COL_CM:_COL_CM + 1]
        b_col = t[:, _COL_B:_COL_B + 1]
        w_col = t[:, _COL_W:_COL_W + 1]
        r_row = rows_ref[d, pl.ds(_ROW_R * nc + c, 1), :]
        g_row = rows_ref[d, pl.ds(_ROW_G * nc + c, 1), :]
        ml_row = rows_ref[d, pl.ds(_ROW_ML * nc + c, 1), :]

        u = jnp.maximum(cm_col, m)
        wgt = jnp.exp(jnp.where(masks[d], r_row - u, -jnp.inf))
        qk = lax.dot_general(q, k, contract_last, preferred_element_type=F32)
        s = ((qk * scale) * wgt).astype(BF16)
        inter_w = jnp.exp(m - u)
        num = (jnp.dot(s, v, preferred_element_type=F32)
               + jnp.concatenate([inter_w, inter_w], axis=1)
               * jnp.dot(q, c_prev.astype(BF16), preferred_element_type=F32))
        den = (jnp.dot(s, ones, preferred_element_type=F32)
               + inter_w * jnp.dot(q, n_prev.astype(BF16), preferred_element_type=F32))
        rinv = 1.0 / jnp.maximum(jnp.abs(den), jnp.exp(-(b_col + u)))
        h = num * jnp.concatenate([rinv, rinv], axis=1)

        kw = (k.astype(F32) * w_col).astype(BF16)
        kv = lax.dot_general(kw, v, contract_rows, preferred_element_type=F32)
        kn = lax.dot_general(kw, ones, contract_rows, preferred_element_type=F32)
        m_new = jnp.maximum(g_row + m, ml_row)
        a1 = jnp.exp(g_row + m - m_new)[:, 0:1]
        a2 = jnp.exp(ml_row - m_new)[:, 0:1]
        return h, (m_new, a1 * c_prev + a2 * kv, a1 * n_prev + a2 * kn)

    def step(c, carry):
        st_f, st_b = carry
        cb = nc - 1 - c
        h_f, st_f = chunk(0, c, st_f)
        h_b, st_b = chunk(1, cb, st_b)
        hf_ref[pl.ds(pl.multiple_of(c * L, L), L), :] = h_f
        hb_ref[pl.ds(pl.multiple_of(cb * L, L), L), :] = h_b
        return st_f, st_b

    st0 = (jnp.zeros((1, L), F32), jnp.zeros((M_QK, M_V), F32), jnp.zeros((M_QK, LANES), F32))
    lax.fori_loop(0, nc, step, (st0, st0), unroll=8)

    def finish(c, carry):
        rows = pl.ds(pl.multiple_of(c * L, L), L)
        hm = hf_ref[rows, :] + hb_ref[rows, :]
        ms = jnp.mean(hm * hm, axis=-1, keepdims=True)
        y = hm * lax.rsqrt(ms + EPS) * nw_ref[0]
        o_ref[rows, :] = (y * jax.nn.sigmoid(mo_ref[rows, :].astype(F32))).astype(BF16)
        return carry

    lax.fori_loop(0, nc, finish, 0, unroll=4)


def _mlstm(wide, gates_t, m_out_norm, batch, seq):
    nc = seq // M_CHUNK
    t = batch * seq
    body = functools.partial(_mlstm_body, seq=seq)
    return pl.pallas_call(
        body,
        grid=(batch, M_HEADS),
        in_specs=[
            pl.BlockSpec((seq, M_QK), lambda b, h: (b, OFF_MQ // M_QK + h)),
            pl.BlockSpec((seq, M_QK), lambda b, h: (b, OFF_MK // M_QK + h)),
            pl.BlockSpec((seq, M_V), lambda b, h: (b, OFF_MV // M_V + h)),
            pl.BlockSpec((seq, M_V), lambda b, h: (b, OFF_MO // M_V + h)),
            pl.BlockSpec((1, 1, 4, nc, M_CHUNK), lambda b, h: (b, h, 0, 0, 0)),
            pl.BlockSpec((1, 1, M_V), lambda b, h: (h, 0, 0)),
        ],
        out_specs=pl.BlockSpec((seq, M_V), lambda b, h: (b, h)),
        out_shape=jax.ShapeDtypeStruct((t, M_HEADS * M_V), BF16),
        scratch_shapes=[
            pltpu.VMEM((seq, M_V), F32),
            pltpu.VMEM((seq, M_V), F32),
            pltpu.VMEM((2, 3 * nc, M_CHUNK), F32),
            pltpu.VMEM((2, M_CHUNK, LANES), F32),
        ],
        compiler_params=pltpu.CompilerParams(
            dimension_semantics=("parallel", "parallel"), vmem_limit_bytes=VMEM_LIMIT),
        name="mlstm",
    )(wide, wide, wide, wide, gates_t, m_out_norm)


def _qk_prep(xb, ctab, stab, gmat, pmat):
    x = xb.astype(F32)
    ms = jnp.dot((x * x).astype(BF16), gmat, preferred_element_type=F32)
    xp = jnp.dot(xb, pmat, preferred_element_type=F32)
    return (x * ctab + xp * stab) * lax.rsqrt(ms + EPS)


V_ROWS = A_DV + 16
LOG2E = math.log2(math.e)
SAFE_BOUND_LOG2 = 48.0
BOUND_SLACK = 1.02


def _attn_body(aq_ref, ak_ref, av_ref, cq_ref, sq_ref, ck_ref, sk_ref, nw_ref, gm_ref, pm_ref, cmaskc_ref,
               lam_ref, on_ref, o_ref, kp_ref, vt_ref, qt_ref, m_ref, ot_ref, *, seq, tq, kc, lam_init):
    gm = gm_ref[...]
    pm = pm_ref[...]
    prep = min(512, seq)
    tiles_per_prep = prep // tq
    nq = seq // tq

    for i in range(seq // prep):
        rows = slice(i * prep, (i + 1) * prep)
        kp_ref[rows, :] = _qk_prep(ak_ref[rows, :], ck_ref[rows, :], sk_ref[rows, :], gm, pm).astype(BF16)
        vt_ref[0:A_DV, rows] = av_ref[rows, :].T
    vt_ref[A_DV:V_ROWS, :] = jnp.ones((V_ROWS - A_DV, seq), BF16)

    for i in range(seq // prep):
        rows = slice(i * prep, (i + 1) * prep)
        qt = _qk_prep(aq_ref[rows, :], cq_ref[rows, :], sq_ref[rows, :], gm, pm).astype(BF16).T
        for c in range(2):
            qct = qt * cmaskc_ref[:, c:c + 1]
            for k in range(tiles_per_prep):
                qt_ref[i * tiles_per_prep + k, c] = qct[:, k * tq:(k + 1) * tq]

    nw = jnp.abs(nw_ref[...])
    bound = (jnp.max(nw[0:1], axis=1, keepdims=True) * jnp.max(nw[1:2], axis=1, keepdims=True)
             * (A_DH * A_DH ** -0.5 * LOG2E * BOUND_SLACK))
    m_ref[...] = jnp.broadcast_to(bound.reshape(1, 1, 1), m_ref.shape)

    @pl.when(bound[0, 0] > SAFE_BOUND_LOG2)
    def _():
        def exact_max(qi, carry):
            for c in range(2):
                mx = jnp.full((1, tq), -jnp.inf, F32)
                for j in range(seq // kc):
                    s = jnp.dot(kp_ref[j * kc:(j + 1) * kc, :], qt_ref[qi, c], preferred_element_type=F32)
                    mx = jnp.maximum(mx, jnp.max(s, axis=0, keepdims=True))
                m_ref[qi, c:c + 1, :] = mx
            return carry

        lax.fori_loop(0, nq, exact_max, 0)

    lv = lam_ref[...]
    lam = (jnp.exp(jnp.sum(lv[0:1] * lv[1:2], axis=1, keepdims=True))
           - jnp.exp(jnp.sum(lv[2:3] * lv[3:4], axis=1, keepdims=True)) + lam_init)

    def q_tile(qi, carry):
        acc = [jnp.zeros((V_ROWS, tq), F32) for _ in range(2)]
        for j in range(seq // kc):
            kblk = kp_ref[j * kc:(j + 1) * kc, :]
            vblk = vt_ref[:, j * kc:(j + 1) * kc]
            for c in range(2):
                s = jnp.dot(kblk, qt_ref[qi, c], preferred_element_type=F32)
                p = jnp.exp2(s - m_ref[qi, c:c + 1, :]).astype(BF16)
                acc[c] = acc[c] + jnp.dot(vblk, p, preferred_element_type=F32)
        ot_ref[qi] = (acc[0][0:A_DV] * (1.0 / acc[0][A_DV:A_DV + 1])
                      - acc[1][0:A_DV] * (lam / acc[1][A_DV:A_DV + 1]))
        return carry

    lax.fori_loop(0, nq, q_tile, 0, unroll=2)

    for i in range(seq // tq):
        ot = ot_ref[i]
        ms = jnp.mean(ot * ot, axis=0, keepdims=True)
        y = ot * lax.rsqrt(ms + EPS) * (on_ref[...] * (1.0 - lam_init))
        o_ref[i * tq:(i + 1) * tq, :] = y.astype(BF16).T


def _attention(wide, qtabs, ktabs, nw, gmat, pmat, cmaskc, lamv, on, batch, seq, tq, lam_init):
    t = batch * seq
    body = functools.partial(_attn_body, seq=seq, tq=tq, kc=min(1024, seq), lam_init=lam_init)
    const = lambda b, h: (0, 0)
    table = pl.BlockSpec((seq, LANES), const)
    return pl.pallas_call(
        body,
        grid=(batch, A_HEADS),
        in_specs=[
            pl.BlockSpec((seq, LANES), lambda b, h: (b, OFF_AQ // LANES + h)),
            pl.BlockSpec((seq, LANES), lambda b, h: (b, OFF_AK // LANES + h)),
            pl.BlockSpec((seq, LANES), lambda b, h: (b, OFF_AV // LANES + h)),
            table, table, table, table,
            pl.BlockSpec((2, LANES), const),
            pl.BlockSpec((LANES, LANES), const),
            pl.BlockSpec((LANES, LANES), const),
            pl.BlockSpec((LANES, 2), const),
            pl.BlockSpec((4, A_DH), const),
            pl.BlockSpec((A_DV, 1), const),
        ],
        out_specs=pl.BlockSpec((seq, A_DV), lambda b, h: (b, h)),
        out_shape=jax.ShapeDtypeStruct((t, A_HEADS * A_DV), BF16),
        scratch_shapes=[
            pltpu.VMEM((seq, LANES), BF16),
            pltpu.VMEM((V_ROWS, seq), BF16),
            pltpu.VMEM((seq // tq, 2, LANES, tq), BF16),
            pltpu.VMEM((seq // tq, SUBLANES, tq), F32),
            pltpu.VMEM((seq // tq, A_DV, tq), F32),
        ],
        compiler_params=pltpu.CompilerParams(
            dimension_semantics=("parallel", "parallel"), vmem_limit_bytes=VMEM_LIMIT),
        name="diffattn",
    )(wide, wide, wide, qtabs[0], qtabs[1], ktabs[0], ktabs[1], nw, gmat, pmat, cmaskc, lamv, on)


def _merge_body(x_ref, ha_ref, hb_ref, ga_ref, gb_ref, pa_ref, pb_ref, wo_ref, o_ref):
    ya = jnp.dot(ha_ref[...], pa_ref[...], preferred_element_type=F32)
    yb = jnp.dot(hb_ref[...], pb_ref[...], preferred_element_type=F32)
    y = (jax.nn.sigmoid(ga_ref[...].astype(F32)) * ya + jax.nn.sigmoid(gb_ref[...].astype(F32)) * yb)
    o_ref[...] = x_ref[...] + jnp.dot(y.astype(BF16), wo_ref[...], preferred_element_type=F32)


def _merge(x2, ha, hb, wide, pa, pb, wo, tm):
    t = x2.shape[0]
    row = lambda i: (i, 0)
    const = lambda i: (0, 0)
    wspec = pl.BlockSpec((D_MODEL, D_MODEL), const)
    return pl.pallas_call(
        _merge_body,
        grid=(t // tm,),
        in_specs=[
            pl.BlockSpec((tm, D_MODEL), row),
            pl.BlockSpec((tm, D_MODEL), row),
            pl.BlockSpec((tm, D_MODEL), row),
            pl.BlockSpec((tm, D_MODEL), lambda i: (i, OFF_GA // D_MODEL)),
            pl.BlockSpec((tm, D_MODEL), lambda i: (i, OFF_GB // D_MODEL)),
            wspec, wspec, wspec,
        ],
        out_specs=pl.BlockSpec((tm, D_MODEL), row),
        out_shape=jax.ShapeDtypeStruct((t, D_MODEL), F32),
        compiler_params=pltpu.CompilerParams(
            dimension_semantics=("parallel",), vmem_limit_bytes=VMEM_LIMIT),
        name="merge",
    )(x2, ha, hb, wide, wide, pa, pb, wo)


FFN_CHUNK = 256
N_FFN_CHUNKS = D_FF // FFN_CHUNK


FFN_ROWS = 128


def _ffn_body(xc_ref, xp_ref, xn_ref, g_ref, wup_ref, cwb_ref, wdn_ref, o_ref, xe_ref, h0_ref, h1_ref, *, ts):
    i = pl.program_id(1)
    last = pl.num_programs(1) - 1
    halo = SUBLANES

    def nrm(x):
        ms = jnp.mean(x * x, axis=-1, keepdims=True)
        return x * lax.rsqrt(ms + EPS) * g_ref[...]

    xe_ref[0:halo, :] = (nrm(xp_ref[...]) * (i > 0).astype(F32)).astype(BF16)
    xe_ref[halo:halo + ts, :] = nrm(xc_ref[...]).astype(BF16)
    xe_ref[halo + ts:, :] = (nrm(xn_ref[...]) * (i < last).astype(F32)).astype(BF16)
    o_ref[...] = xc_ref[...]

    h_refs = (h0_ref, h1_ref)

    def conv(u, cwb):
        return (u[halo - 1:halo - 1 + FFN_ROWS] * cwb[0:1] + u[halo:halo + FFN_ROWS] * cwb[1:2]
                + u[halo + 1:halo + 1 + FFN_ROWS] * cwb[2:3] + cwb[3:4])

    def up_gate(c, slot):
        cwa = cwb_ref[c]
        cwg = cwb_ref[N_FFN_CHUNKS + c]
        for r in range(ts // FFN_ROWS):
            xe = xe_ref[r * FFN_ROWS:r * FFN_ROWS + FFN_ROWS + 2 * halo, :]
            ua = jnp.dot(xe, wup_ref[c], preferred_element_type=F32)
            ug = jnp.dot(xe, wup_ref[N_FFN_CHUNKS + c], preferred_element_type=F32)
            h_refs[slot][r * FFN_ROWS:(r + 1) * FFN_ROWS, :] = (
                jax.nn.gelu(conv(ug, cwg)) * conv(ua, cwa)).astype(BF16)

    def down(c, slot):
        o_ref[...] += jnp.dot(h_refs[slot][...], wdn_ref[c], preferred_element_type=F32)

    up_gate(0, 0)

    def pair(cc, carry):
        c = 2 * cc
        up_gate(c + 1, 1)
        down(c, 0)
        up_gate(c + 2, 0)
        down(c + 1, 1)
        return carry

    assert N_FFN_CHUNKS % 2 == 1
    lax.fori_loop(0, N_FFN_CHUNKS // 2, pair, 0)
    down(N_FFN_CHUNKS - 1, 0)


def _ffn(x1, g, wup, cwb, wdn, batch, seq, ts):
    t = batch * seq
    ns = seq // ts
    hb = ts // SUBLANES
    nblk8 = t // SUBLANES
    body = functools.partial(_ffn_body, ts=ts)
    const2 = lambda b, i: (0, 0)
    const3 = lambda b, i: (0, 0, 0)
    return pl.pallas_call(
        body,
        grid=(batch, ns),
        in_specs=[
            pl.BlockSpec((ts, D_MODEL), lambda b, i: (b * ns + i, 0)),
            pl.BlockSpec((SUBLANES, D_MODEL), lambda b, i: (jnp.maximum((b * ns + i) * hb - 1, 0), 0)),
            pl.BlockSpec((SUBLANES, D_MODEL), lambda b, i: (jnp.minimum((b * ns + i + 1) * hb, nblk8 - 1), 0)),
            pl.BlockSpec((1, D_MODEL), const2),
            pl.BlockSpec((2 * N_FFN_CHUNKS, D_MODEL, FFN_CHUNK), const3, pipeline_mode=pl.Buffered(1)),
            pl.BlockSpec((2 * N_FFN_CHUNKS, 4, FFN_CHUNK), const3),
            pl.BlockSpec((N_FFN_CHUNKS, FFN_CHUNK, D_MODEL), const3, pipeline_mode=pl.Buffered(1)),
        ],
        out_specs=pl.BlockSpec((ts, D_MODEL), lambda b, i: (b * ns + i, 0)),
        out_shape=jax.ShapeDtypeStruct((t, D_MODEL), F32),
        scratch_shapes=[
            pltpu.VMEM((ts + 2 * SUBLANES, D_MODEL), BF16),
            pltpu.VMEM((ts, FFN_CHUNK), BF16),
            pltpu.VMEM((ts, FFN_CHUNK), BF16),
        ],
        compiler_params=pltpu.CompilerParams(
            dimension_semantics=("parallel", "arbitrary"), vmem_limit_bytes=VMEM_LIMIT),
        name="convffn",
    )(x1, x1, x1, g, wup, cwb, wdn)


def _rope_lane_tables(seq):
    half, _, freq, _ = _head_lanes()
    pos = jnp.arange(seq, dtype=F32)
    inv = ROPE_THETA ** (-jnp.arange(0, A_DH, 2, dtype=F32) / A_DH)
    ang = pos[:, None] * inv[None, :]
    cos, sin = jnp.cos(ang), jnp.sin(ang)
    sign = jnp.asarray(np.where(half == 0, -1.0, 1.0), F32)
    return cos[:, freq], sin[:, freq] * sign[None, :]


def kernel(x, norm1, w_in, gate_bias, m_out_norm, q_norm, k_norm, lam_q1, lam_k1, lam_q2, lam_k2,
           a_out_norm, p_a, p_b, w_o, norm2, w_up, conv_w, conv_b, w_down):
    batch, seq, _ = x.shape
    t = batch * seq
    depth = norm1.shape[0]
    nc = seq // M_CHUNK
    tm = min(1024, t)
    tq = min(256, seq)
    ts = min(1024, seq)

    half, comp, freq, partner = _head_lanes()
    lane_norm = np.stack([half * (A_DH // 2) + freq, (1 - half) * (A_DH // 2) + freq])
    gmat = jnp.asarray((comp[:, None] == comp[None, :]).astype(np.float32) / A_DH, BF16)
    pmat = jnp.asarray((np.arange(LANES)[:, None] == partner[None, :]).astype(np.float32), BF16)
    cmaskc = jnp.asarray(np.stack([comp == 0, comp == 1], axis=1).astype(np.float32), BF16)
    tabs = _rope_lane_tables(seq)

    x2 = x.reshape(t, D_MODEL)
    for l in range(depth):
        lam_init = 0.8 - 0.6 * math.exp(-0.3 * l)
        w_wide = jnp.concatenate([w_in[l][:, :_SRC_MG], w_in[l][:, _SRC_MG + N_GATE:]], axis=1).astype(BF16)
        w_gate = jnp.pad(w_in[l][:, _SRC_MG:_SRC_MG + N_GATE], ((0, 0), (0, LANES - N_GATE))).astype(BF16)
        b_gate = jnp.pad(gate_bias[l].astype(F32), (0, LANES - N_GATE)).reshape(1, LANES)

        wide, gates_t = _inproj(x2, norm1[l].reshape(1, D_MODEL), w_wide, w_gate, b_gate, min(512, t), 1024)
        gates_t = (gates_t[:N_GATE].reshape(4, M_HEADS, batch, nc, M_CHUNK).transpose(2, 1, 0, 3, 4))

        h_a = _mlstm(wide, gates_t, m_out_norm[l].reshape(M_HEADS, 1, M_V), batch, seq)

        lamv = jnp.stack([lam_q1[l], lam_k1[l], lam_q2[l], lam_k2[l]]).astype(F32)
        qw, kw = q_norm[l][lane_norm].astype(F32), k_norm[l][lane_norm].astype(F32)
        qscale = A_DH ** -0.5 * LOG2E
        qtabs = (tabs[0] * (qw[0] * qscale), tabs[1] * (qw[1] * qscale))
        ktabs = (tabs[0] * kw[0], tabs[1] * kw[1])
        h_b = _attention(wide, qtabs, ktabs, jnp.stack([qw[0], kw[0]]), gmat, pmat, cmaskc, lamv,
                         a_out_norm[l].reshape(A_DV, 1), batch, seq, tq, lam_init)

        x2 = _merge(x2, h_a, h_b, wide, p_a[l].astype(BF16), p_b[l].astype(BF16), w_o[l].astype(BF16), tm)

        wup3 = w_up[l].astype(BF16).reshape(D_MODEL, 2 * N_FFN_CHUNKS, FFN_CHUNK).transpose(1, 0, 2)
        cwb = jnp.concatenate([conv_w[l].reshape(CONV_W, 2 * D_FF), conv_b[l].reshape(1, 2 * D_FF)], axis=0)
        cwb = cwb.astype(F32).reshape(CONV_W + 1, 2 * N_FFN_CHUNKS, FFN_CHUNK).transpose(1, 0, 2)
        wdn3 = w_down[l].astype(BF16).reshape(N_FFN_CHUNKS, FFN_CHUNK, D_MODEL)
        x2 = _ffn(x2, norm2[l].reshape(1, D_MODEL), wup3, cwb, wdn3, batch, seq, ts)
    return x2.reshape(batch, seq, D_MODEL)
```

```python
import functools
import math

import numpy as np
import jax
import jax.numpy as jnp
from jax import lax
from jax.experimental import pallas as pl
from jax.experimental.pallas import tpu as pltpu

F32 = jnp.float32
BF16 = jnp.bfloat16

D_MODEL = 1024
M_HEADS = 4
M_QK = 128
M_V = 256
M_CHUNK = 128
A_HEADS = 8
A_DH = 64
A_DV = 128
ROPE_THETA = 10000.0
D_FF = 2816
CONV_W = 3
EPS = 1e-6
N_GATE = 4 * M_HEADS

LANES = 128
SUBLANES = 8
VMEM_LIMIT = 60 * 1024 * 1024

_SRC_MG = 3072
OFF_MQ, OFF_MK, OFF_MV, OFF_MO = 0, 512, 1024, 2048
OFF_AQ, OFF_AK, OFF_AV, OFF_GA, OFF_GB = 3072, 4096, 5120, 6144, 7168
D_WIDE = 8192


def _head_lanes():
    lane = np.arange(LANES)
    comp, rem = np.divmod(lane, A_DH)
    half, freq = np.divmod(rem, A_DH // 2)
    partner = comp * A_DH + (1 - half) * (A_DH // 2) + freq
    return half, comp, freq, partner


def _inproj_body(x_ref, g_ref, w_ref, wg_ref, bg_ref, o_ref, ogt_ref, xn_ref, *, tn):
    x = x_ref[...]
    ms = jnp.mean(x * x, axis=-1, keepdims=True)
    xn = (x * lax.rsqrt(ms + EPS) * g_ref[...]).astype(BF16)
    xn_ref[...] = xn
    og = jnp.dot(xn, wg_ref[...], preferred_element_type=F32) + bg_ref[...]
    ogt_ref[...] = og.T
    for n in range(D_WIDE // tn):
        cols = slice(n * tn, (n + 1) * tn)
        o_ref[:, cols] = jnp.dot(xn_ref[...], w_ref[:, cols], preferred_element_type=F32).astype(BF16)


def _inproj(x2, g, w, wg, bg, tm, tn):
    t = x2.shape[0]
    const = lambda i: (0, 0)
    return pl.pallas_call(
        functools.partial(_inproj_body, tn=tn),
        grid=(t // tm,),
        in_specs=[
            pl.BlockSpec((tm, D_MODEL), lambda i: (i, 0)),
            pl.BlockSpec((1, D_MODEL), const),
            pl.BlockSpec((D_MODEL, D_WIDE), const, pipeline_mode=pl.Buffered(1)),
            pl.BlockSpec((D_MODEL, LANES), const),
            pl.BlockSpec((1, LANES), const),
        ],
        out_specs=[
            pl.BlockSpec((tm, D_WIDE), lambda i: (i, 0)),
            pl.BlockSpec((LANES, tm), lambda i: (0, i)),
        ],
        out_shape=[
            jax.ShapeDtypeStruct((t, D_WIDE), BF16),
            jax.ShapeDtypeStruct((LANES, t), F32),
        ],
        scratch_shapes=[pltpu.VMEM((tm, D_MODEL), BF16)],
        compiler_params=pltpu.CompilerParams(
            dimension_semantics=("parallel",), vmem_limit_bytes=VMEM_LIMIT),
        name="inproj",
    )(x2, g, w, wg, bg)


def _lane_scan(x, op, ident, reverse):
    n = x.shape[1]
    lane = lax.broadcasted_iota(jnp.int32, x.shape, 1)
    sh = 1
    while sh < n:
        if reverse:
            y = jnp.where(lane < n - sh, pltpu.roll(x, n - sh, 1), ident)
        else:
            y = jnp.where(lane >= sh, pltpu.roll(x, sh, 1), ident)
        x = op(x, y)
        sh *= 2
    return x


_ROW_R, _ROW_G, _ROW_ML = 0, 1, 2
_COL_CM, _COL_B, _COL_W = 0, 32, 64


def _mlstm_body(q_ref, k_ref, v_ref, mo_ref, gt_ref, nw_ref, o_ref,
                hf_ref, hb_ref, rows_ref, cols_ref, *, seq):
    L = M_CHUNK
    nc = seq // L
    scale = M_QK ** -0.5
    gt = gt_ref[0, 0]

    for d in range(2):
        reverse = d == 1
        i_pre = gt[2 * d]
        logf = jax.nn.log_sigmoid(gt[2 * d + 1])
        b = _lane_scan(logf, jnp.add, 0.0, reverse)
        g = jnp.sum(logf, axis=1, keepdims=True)
        a = g - b + i_pre
        m_loc = jnp.max(a, axis=1, keepdims=True)
        w = jnp.exp(a - m_loc) * scale
        r = i_pre - b
        cm = _lane_scan(r, jnp.maximum, -jnp.inf, reverse)
        rows_ref[d, _ROW_R * nc:(_ROW_R + 1) * nc, :] = r
        rows_ref[d, _ROW_G * nc:(_ROW_G + 1) * nc, :] = jnp.broadcast_to(g, (nc, L))
        rows_ref[d, _ROW_ML * nc:(_ROW_ML + 1) * nc, :] = jnp.broadcast_to(m_loc, (nc, L))
        pad = [jnp.zeros((32 - nc, L), F32)] if nc < 32 else []
        stack = jnp.concatenate([cm] + pad + [b] + pad + [w] + pad + [jnp.zeros((32, L), F32)], axis=0)
        cols_ref[d] = stack.T

    row_i = lax.broadcasted_iota(jnp.int32, (L, L), 0)
    col_i = lax.broadcasted_iota(jnp.int32, (L, L), 1)
    masks = (col_i <= row_i, col_i >= row_i)
    ones = jnp.ones((L, L), BF16)
    contract_rows = (((0,), (0,)), ((), ()))
    contract_last = (((1,), (1,)), ((), ()))

    def chunk(d, c, state):
        m, c_prev, n_prev = state
        rs = pl.multiple_of(c * L, L)
        q = q_ref[pl.ds(rs, L), :]
        k = k_ref[pl.ds(rs, L), :]
        v = v_ref[pl.ds(rs, L), :]
        t = pltpu.roll(cols_ref[d], jnp.bitwise_and(LANES - c, LANES - 1), 1)
        cm_col = t[:, _COL_CM:_COL_CM + 1]
        b_col = t[:, _COL_B:_COL_B + 1]
        w_col = t[:, _COL_W:_COL_W + 1]
        r_row = rows_ref[d, pl.ds(_ROW_R * nc + c, 1), :]
        g_row = rows_ref[d, pl.ds(_ROW_G * nc + c, 1), :]
        ml_row = rows_ref[d, pl.ds(_ROW_ML * nc + c, 1), :]

        u = jnp.maximum(cm_col, m)
        wgt = jnp.exp(jnp.where(masks[d], r_row - u, -jnp.inf))
        qk = lax.dot_general(q, k, contract_last, preferred_element_type=F32)
        s = ((qk * scale) * wgt).astype(BF16)
        inter_w = jnp.exp(m - u)
        num = (jnp.dot(s, v, preferred_element_type=F32)
               + jnp.concatenate([inter_w, inter_w], axis=1)
               * jnp.dot(q, c_prev.astype(BF16), preferred_element_type=F32))
        den = (jnp.dot(s, ones, preferred_element_type=F32)
               + inter_w * jnp.dot(q, n_prev.astype(BF16), preferred_element_type=F32))
        rinv = 1.0 / jnp.maximum(jnp.abs(den), jnp.exp(-(b_col + u)))
        h = num * jnp.concatenate([rinv, rinv], axis=1)

        kw = (k.astype(F32) * w_col).astype(BF16)
        kv = lax.dot_general(kw, v, contract_rows, preferred_element_type=F32)
        kn = lax.dot_general(kw, ones, contract_rows, preferred_element_type=F32)
        m_new = jnp.maximum(g_row + m, ml_row)
        a1 = jnp.exp(g_row + m - m_new)[:, 0:1]
        a2 = jnp.exp(ml_row - m_new)[:, 0:1]
        return h, (m_new, a1 * c_prev + a2 * kv, a1 * n_prev + a2 * kn)

    def step(c, carry):
        st_f, st_b = carry
        cb = nc - 1 - c
        h_f, st_f = chunk(0, c, st_f)
        h_b, st_b = chunk(1, cb, st_b)
        hf_ref[pl.ds(pl.multiple_of(c * L, L), L), :] = h_f
        hb_ref[pl.ds(pl.multiple_of(cb * L, L), L), :] = h_b
        return st_f, st_b

    st0 = (jnp.zeros((1, L), F32), jnp.zeros((M_QK, M_V), F32), jnp.zeros((M_QK, LANES), F32))
    lax.fori_loop(0, nc, step, (st0, st0), unroll=8)

    def finish(c, carry):
        rows = pl.ds(pl.multiple_of(c * L, L), L)
        hm = hf_ref[rows, :] + hb_ref[rows, :]
        ms = jnp.mean(hm * hm, axis=-1, keepdims=True)
        y = hm * lax.rsqrt(ms + EPS) * nw_ref[0]
        o_ref[rows, :] = (y * jax.nn.sigmoid(mo_ref[rows, :].astype(F32))).astype(BF16)
        return carry

    lax.fori_loop(0, nc, finish, 0, unroll=4)


def _mlstm(wide, gates_t, m_out_norm, batch, seq):
    nc = seq // M_CHUNK
    t = batch * seq
    body = functools.partial(_mlstm_body, seq=seq)
    return pl.pallas_call(
        body,
        grid=(batch, M_HEADS),
        in_specs=[
            pl.BlockSpec((seq, M_QK), lambda b, h: (b, OFF_MQ // M_QK + h)),
            pl.BlockSpec((seq, M_QK), lambda b, h: (b, OFF_MK // M_QK + h)),
            pl.BlockSpec((seq, M_V), lambda b, h: (b, OFF_MV // M_V + h)),
            pl.BlockSpec((seq, M_V), lambda b, h: (b, OFF_MO // M_V + h)),
            pl.BlockSpec((1, 1, 4, nc, M_CHUNK), lambda b, h: (b, h, 0, 0, 0)),
            pl.BlockSpec((1, 1, M_V), lambda b, h: (h, 0, 0)),
        ],
        out_specs=pl.BlockSpec((seq, M_V), lambda b, h: (b, h)),
        out_shape=jax.ShapeDtypeStruct((t, M_HEADS * M_V), BF16),
        scratch_shapes=[
            pltpu.VMEM((seq, M_V), F32),
            pltpu.VMEM((seq, M_V), F32),
            pltpu.VMEM((2, 3 * nc, M_CHUNK), F32),
            pltpu.VMEM((2, M_CHUNK, LANES), F32),
        ],
        compiler_params=pltpu.CompilerParams(
            dimension_semantics=("parallel", "parallel"), vmem_limit_bytes=VMEM_LIMIT),
        name="mlstm",
    )(wide, wide, wide, wide, gates_t, m_out_norm)


def _qk_prep(xb, ctab, stab, gmat, pmat):
    x = xb.astype(F32)
    ms = jnp.dot((x * x).astype(BF16), gmat, preferred_element_type=F32)
    xp = jnp.dot(xb, pmat, preferred_element_type=F32)
    return (x * ctab + xp * stab) * lax.rsqrt(ms + EPS)


LOG2E = math.log2(math.e)
SAFE_BOUND_LOG2 = 48.0
BOUND_SLACK = 1.02


def _attn_body(aq_ref, ak_ref, av_ref, cq_ref, sq_ref, ck_ref, sk_ref, nw_ref, gm_ref, pm_ref, cmaskc_ref,
               lam_ref, on_ref, o_ref, kp_ref, vt_ref, qt_ref, m_ref, ot_ref, p_ref, *, seq, tq, kc, lam_init):
    gm = gm_ref[...]
    pm = pm_ref[...]
    prep = min(512, seq)
    piece = min(prep, tq)
    nq = seq // tq

    for i in range(seq // prep):
        rows = slice(i * prep, (i + 1) * prep)
        kp_ref[rows, :] = _qk_prep(ak_ref[rows, :], ck_ref[rows, :], sk_ref[rows, :], gm, pm).astype(BF16)
        vt_ref[:, rows] = av_ref[rows, :].T

    for i in range(seq // prep):
        rows = slice(i * prep, (i + 1) * prep)
        qt = _qk_prep(aq_ref[rows, :], cq_ref[rows, :], sq_ref[rows, :], gm, pm).astype(BF16).T
        for c in range(2):
            qct = qt * cmaskc_ref[:, c:c + 1]
            for k in range(prep // piece):
                tile, off = divmod(i * prep + k * piece, tq)
                qt_ref[tile, c, :, off:off + piece] = qct[:, k * piece:(k + 1) * piece]

    nw = jnp.abs(nw_ref[...])
    bound = (jnp.max(nw[0:1], axis=1, keepdims=True) * jnp.max(nw[1:2], axis=1, keepdims=True)
             * (A_DH * A_DH ** -0.5 * LOG2E * BOUND_SLACK))
    m_ref[...] = jnp.broadcast_to(bound.reshape(1, 1, 1), m_ref.shape)

    @pl.when(bound[0, 0] > SAFE_BOUND_LOG2)
    def _():
        def exact_max(qi, carry):
            for c in range(2):
                mx = jnp.full((1, tq), -jnp.inf, F32)
                for j in range(seq // kc):
                    s = jnp.dot(kp_ref[j * kc:(j + 1) * kc, :], qt_ref[qi, c], preferred_element_type=F32)
                    mx = jnp.maximum(mx, jnp.max(s, axis=0, keepdims=True))
                m_ref[qi, c:c + 1, :] = mx
            return carry

        lax.fori_loop(0, nq, exact_max, 0)

    lv = lam_ref[...]
    lam = (jnp.exp(jnp.sum(lv[0:1] * lv[1:2], axis=1, keepdims=True))
           - jnp.exp(jnp.sum(lv[2:3] * lv[3:4], axis=1, keepdims=True)) + lam_init)

    def q_tile(qi, carry):
        lpart = [jnp.zeros((SUBLANES, tq), F32) for _ in range(2)]
        for j in range(seq // kc):
            rows = slice(j * kc, (j + 1) * kc)
            kblk = kp_ref[rows, :]
            for c in range(2):
                s = jnp.dot(kblk, qt_ref[qi, c], preferred_element_type=F32)
                p = jnp.exp2(s - m_ref[qi, c:c + 1, :])
                lpart[c] = lpart[c] + jnp.sum(p.reshape(kc // SUBLANES, SUBLANES, tq), axis=0)
                p_ref[c, rows, :] = p.astype(BF16)
        lsum = [jnp.sum(x, axis=0, keepdims=True) for x in lpart]
        rho = (lam * lsum[0] / lsum[1]).astype(BF16)
        acc = jnp.zeros((A_DV, tq), F32)
        for j in range(seq // kc):
            rows = slice(j * kc, (j + 1) * kc)
            w = p_ref[0, rows, :] - rho * p_ref[1, rows, :]
            acc = acc + jnp.dot(vt_ref[:, rows], w, preferred_element_type=F32)
        ot_ref[qi] = acc * (1.0 / lsum[0])
        return carry

    lax.fori_loop(0, nq, q_tile, 0, unroll=2)

    for i in range(seq // tq):
        ot = ot_ref[i]
        ms = jnp.mean(ot * ot, axis=0, keepdims=True)
        y = ot * lax.rsqrt(ms + EPS) * (on_ref[...] * (1.0 - lam_init))
        o_ref[i * tq:(i + 1) * tq, :] = y.astype(BF16).T


def _attention(wide, qtabs, ktabs, nw, gmat, pmat, cmaskc, lamv, on, batch, seq, tq, lam_init):
    t = batch * seq
    body = functools.partial(_attn_body, seq=seq, tq=tq, kc=min(2048, seq), lam_init=lam_init)
    const = lambda b, h: (0, 0)
    table = pl.BlockSpec((seq, LANES), const, pipeline_mode=pl.Buffered(1))
    return pl.pallas_call(
        body,
        grid=(batch, A_HEADS),
        in_specs=[
            pl.BlockSpec((seq, LANES), lambda b, h: (b, OFF_AQ // LANES + h)),
            pl.BlockSpec((seq, LANES), lambda b, h: (b, OFF_AK // LANES + h)),
            pl.BlockSpec((seq, LANES), lambda b, h: (b, OFF_AV // LANES + h)),
            table, table, table, table,
            pl.BlockSpec((2, LANES), const),
            pl.BlockSpec((LANES, LANES), const),
            pl.BlockSpec((LANES, LANES), const),
            pl.BlockSpec((LANES, 2), const),
            pl.BlockSpec((4, A_DH), const),
            pl.BlockSpec((A_DV, 1), const),
        ],
        out_specs=pl.BlockSpec((seq, A_DV), lambda b, h: (b, h)),
        out_shape=jax.ShapeDtypeStruct((t, A_HEADS * A_DV), BF16),
        scratch_shapes=[
            pltpu.VMEM((seq, LANES), BF16),
            pltpu.VMEM((A_DV, seq), BF16),
            pltpu.VMEM((seq // tq, 2, LANES, tq), BF16),
            pltpu.VMEM((seq // tq, SUBLANES, tq), F32),
            pltpu.VMEM((seq // tq, A_DV, tq), F32),
            pltpu.VMEM((2, seq, tq), BF16),
        ],
        compiler_params=pltpu.CompilerParams(
            dimension_semantics=("parallel", "parallel"), vmem_limit_bytes=VMEM_LIMIT),
        name="diffattn",
    )(wide, wide, wide, qtabs[0], qtabs[1], ktabs[0], ktabs[1], nw, gmat, pmat, cmaskc, lamv, on)


def _merge_body(x_ref, ha_ref, hb_ref, ga_ref, gb_ref, pa_ref, pb_ref, wo_ref, o_ref):
    ya = jnp.dot(ha_ref[...], pa_ref[...], preferred_element_type=F32)
    yb = jnp.dot(hb_ref[...], pb_ref[...], preferred_element_type=F32)
    y = (jax.nn.sigmoid(ga_ref[...].astype(F32)) * ya + jax.nn.sigmoid(gb_ref[...].astype(F32)) * yb)
    o_ref[...] = x_ref[...] + jnp.dot(y.astype(BF16), wo_ref[...], preferred_element_type=F32)


def _merge(x2, ha, hb, wide, pa, pb, wo, tm):
    t = x2.shape[0]
    row = lambda i: (i, 0)
    const = lambda i: (0, 0)
    wspec = pl.BlockSpec((D_MODEL, D_MODEL), const)
    return pl.pallas_call(
        _merge_body,
        grid=(t // tm,),
        in_specs=[
            pl.BlockSpec((tm, D_MODEL), row),
            pl.BlockSpec((tm, D_MODEL), row),
            pl.BlockSpec((tm, D_MODEL), row),
            pl.BlockSpec((tm, D_MODEL), lambda i: (i, OFF_GA // D_MODEL)),
            pl.BlockSpec((tm, D_MODEL), lambda i: (i, OFF_GB // D_MODEL)),
            wspec, wspec, wspec,
        ],
        out_specs=pl.BlockSpec((tm, D_MODEL), row),
        out_shape=jax.ShapeDtypeStruct((t, D_MODEL), F32),
        compiler_params=pltpu.CompilerParams(
            dimension_semantics=("parallel",), vmem_limit_bytes=VMEM_LIMIT),
        name="merge",
    )(x2, ha, hb, wide, wide, pa, pb, wo)


FFN_CHUNK = 256
N_FFN_CHUNKS = D_FF // FFN_CHUNK


def _ffn_body(xc_ref, xp_ref, xn_ref, g_ref, wup_ref, cwb_ref, wdn_ref, o_ref, xe_ref, h0_ref, h1_ref, *, ts):
    i = pl.program_id(1)
    last = pl.num_programs(1) - 1
    halo = SUBLANES

    def nrm(x):
        ms = jnp.mean(x * x, axis=-1, keepdims=True)
        return x * lax.rsqrt(ms + EPS) * g_ref[...]

    xe_ref[0:halo, :] = (nrm(xp_ref[...]) * (i > 0).astype(F32)).astype(BF16)
    xe_ref[halo:halo + ts, :] = nrm(xc_ref[...]).astype(BF16)
    xe_ref[halo + ts:, :] = (nrm(xn_ref[...]) * (i < last).astype(F32)).astype(BF16)
    o_ref[...] = xc_ref[...]

    h_refs = (h0_ref, h1_ref)

    def conv(u, cwb):
        return (u[halo - 1:halo - 1 + ts] * cwb[0:1] + u[halo:halo + ts] * cwb[1:2]
                + u[halo + 1:halo + 1 + ts] * cwb[2:3] + cwb[3:4])

    def up_gate(c, slot):
        xe = xe_ref[...]
        ua = jnp.dot(xe, wup_ref[c], preferred_element_type=F32)
        ug = jnp.dot(xe, wup_ref[N_FFN_CHUNKS + c], preferred_element_type=F32)
        h_refs[slot][...] = (jax.nn.gelu(conv(ug, cwb_ref[N_FFN_CHUNKS + c]))
                             * conv(ua, cwb_ref[c])).astype(BF16)

    def down(c, slot):
        o_ref[...] += jnp.dot(h_refs[slot][...], wdn_ref[c], preferred_element_type=F32)

    up_gate(0, 0)

    def pair(cc, carry):
        c = 2 * cc
        up_gate(c + 1, 1)
        down(c, 0)
        up_gate(c + 2, 0)
        down(c + 1, 1)
        return carry

    assert N_FFN_CHUNKS % 2 == 1
    lax.fori_loop(0, N_FFN_CHUNKS // 2, pair, 0)
    down(N_FFN_CHUNKS - 1, 0)


def _ffn(x1, g, wup, cwb, wdn, batch, seq, ts):
    t = batch * seq
    ns = seq // ts
    hb = ts // SUBLANES
    nblk8 = t // SUBLANES
    body = functools.partial(_ffn_body, ts=ts)
    const2 = lambda b, i: (0, 0)
    const3 = lambda b, i: (0, 0, 0)
    return pl.pallas_call(
        body,
        grid=(batch, ns),
        in_specs=[
            pl.BlockSpec((ts, D_MODEL), lambda b, i: (b * ns + i, 0)),
            pl.BlockSpec((SUBLANES, D_MODEL), lambda b, i: (jnp.maximum((b * ns + i) * hb - 1, 0), 0)),
            pl.BlockSpec((SUBLANES, D_MODEL), lambda b, i: (jnp.minimum((b * ns + i + 1) * hb, nblk8 - 1), 0)),
            pl.BlockSpec((1, D_MODEL), const2),
            pl.BlockSpec((2 * N_FFN_CHUNKS, D_MODEL, FFN_CHUNK), const3, pipeline_mode=pl.Buffered(1)),
            pl.BlockSpec((2 * N_FFN_CHUNKS, 4, FFN_CHUNK), const3),
            pl.BlockSpec((N_FFN_CHUNKS, FFN_CHUNK, D_MODEL), const3, pipeline_mode=pl.Buffered(1)),
        ],
        out_specs=pl.BlockSpec((ts, D_MODEL), lambda b, i: (b * ns + i, 0)),
        out_shape=jax.ShapeDtypeStruct((t, D_MODEL), F32),
        scratch_shapes=[
            pltpu.VMEM((ts + 2 * SUBLANES, D_MODEL), BF16),
            pltpu.VMEM((ts, FFN_CHUNK), BF16),
            pltpu.VMEM((ts, FFN_CHUNK), BF16),
        ],
        compiler_params=pltpu.CompilerParams(
            dimension_semantics=("parallel", "arbitrary"), vmem_limit_bytes=VMEM_LIMIT),
        name="convffn",
    )(x1, x1, x1, g, wup, cwb, wdn)


def _rope_lane_tables(seq):
    half, _, freq, _ = _head_lanes()
    pos = jnp.arange(seq, dtype=F32)
    inv = ROPE_THETA ** (-jnp.arange(0, A_DH, 2, dtype=F32) / A_DH)
    ang = pos[:, None] * inv[None, :]
    cos, sin = jnp.cos(ang), jnp.sin(ang)
    sign = jnp.asarray(np.where(half == 0, -1.0, 1.0), F32)
    return cos[:, freq], sin[:, freq] * sign[None, :]


def kernel(x, norm1, w_in, gate_bias, m_out_norm, q_norm, k_norm, lam_q1, lam_k1, lam_q2, lam_k2,
           a_out_norm, p_a, p_b, w_o, norm2, w_up, conv_w, conv_b, w_down):
    batch, seq, _ = x.shape
    t = batch * seq
    depth = norm1.shape[0]
    nc = seq // M_CHUNK
    tm = min(1024, t)
    tq = min(1024, seq)
    ts = min(1024, seq)

    half, comp, freq, partner = _head_lanes()
    lane_norm = np.stack([half * (A_DH // 2) + freq, (1 - half) * (A_DH // 2) + freq])
    gmat = jnp.asarray((comp[:, None] == comp[None, :]).astype(np.float32) / A_DH, BF16)
    pmat = jnp.asarray((np.arange(LANES)[:, None] == partner[None, :]).astype(np.float32), BF16)
    cmaskc = jnp.asarray(np.stack([comp == 0, comp == 1], axis=1).astype(np.float32), BF16)
    tabs = _rope_lane_tables(seq)

    x2 = x.reshape(t, D_MODEL)
    for l in range(depth):
        lam_init = 0.8 - 0.6 * math.exp(-0.3 * l)
        w16 = w_in[l].astype(BF16)
        w_wide = jnp.concatenate([w16[:, :_SRC_MG], w16[:, _SRC_MG + N_GATE:]], axis=1)
        w_gate = jnp.pad(w16[:, _SRC_MG:_SRC_MG + N_GATE], ((0, 0), (0, LANES - N_GATE)))
        b_gate = jnp.pad(gate_bias[l].astype(F32), (0, LANES - N_GATE)).reshape(1, LANES)

        wide, gates_t = _inproj(x2, norm1[l].reshape(1, D_MODEL), w_wide, w_gate, b_gate, min(512, t), 1024)
        gates_t = (gates_t[:N_GATE].reshape(4, M_HEADS, batch, nc, M_CHUNK).transpose(2, 1, 0, 3, 4))

        h_a = _mlstm(wide, gates_t, m_out_norm[l].reshape(M_HEADS, 1, M_V), batch, seq)

        lamv = jnp.stack([lam_q1[l], lam_k1[l], lam_q2[l], lam_k2[l]]).astype(F32)
        qw, kw = q_norm[l][lane_norm].astype(F32), k_norm[l][lane_norm].astype(F32)
        qscale = A_DH ** -0.5 * LOG2E
        qtabs = (tabs[0] * (qw[0] * qscale), tabs[1] * (qw[1] * qscale))
        ktabs = (tabs[0] * kw[0], tabs[1] * kw[1])
        h_b = _attention(wide, qtabs, ktabs, jnp.stack([qw[0], kw[0]]), gmat, pmat, cmaskc, lamv,
                         a_out_norm[l].reshape(A_DV, 1), batch, seq, tq, lam_init)

        x2 = _merge(x2, h_a, h_b, wide, p_a[l].astype(BF16), p_b[l].astype(BF16), w_o[l].astype(BF16), tm)

        wup3 = w_up[l].reshape(D_MODEL, 2 * N_FFN_CHUNKS, FFN_CHUNK).transpose(1, 0, 2).astype(BF16)
        cwb = jnp.concatenate([conv_w[l].reshape(CONV_W, 2 * D_FF), conv_b[l].reshape(1, 2 * D_FF)], axis=0)
        cwb = cwb.astype(F32).reshape(CONV_W + 1, 2 * N_FFN_CHUNKS, FFN_CHUNK).transpose(1, 0, 2)
        wdn3 = w_down[l].astype(BF16).reshape(N_FFN_CHUNKS, FFN_CHUNK, D_MODEL)
        x2 = _ffn(x2, norm2[l].reshape(1, D_MODEL), wup3, cwb, wdn3, batch, seq, ts)
    return x2.reshape(batch, seq, D_MODEL)
```
